```python
import jax, jax.numpy as jnp
from jax import lax
import numpy as np


D_MODEL = 1024
BATCH = 16
SEQ = 2048
DEPTH = 4

CHUNK = 64
N_MIXERS = 2
N_SB_LAYERS = (DEPTH + 1) // 2
N_RG_LAYERS = DEPTH // 2

SB_HEADS = 16
SB_HEAD_DIM = D_MODEL // SB_HEADS
Q_BLOCK = 128

RG_WIDTH = D_MODEL
RG_BLOCKS = 16
RG_BLOCK_DIM = RG_WIDTH // RG_BLOCKS
RG_CONV = 4
RG_C = 8.0

D_FF = 4 * D_MODEL
EPS = 1e-6

kernel_name = 'hybrid_stickbreak_rglru_encoder'


def rms_norm(x, g):
    xf = x.astype(jnp.float32)
    y = xf * lax.rsqrt(jnp.mean(xf * xf, axis=-1, keepdims=True) + EPS)
    return (y * g.astype(jnp.float32)).astype(x.dtype)


def stick_breaking_attention(h, w_qkv, w_o):
    B, S, _ = h.shape
    qkv = (h @ w_qkv).reshape(B, S, 3, SB_HEADS, SB_HEAD_DIM)
    q = qkv[:, :, 0].transpose(0, 2, 1, 3)
    k = qkv[:, :, 1].transpose(0, 2, 1, 3)
    v = qkv[:, :, 2].transpose(0, 2, 1, 3)
    scale = SB_HEAD_DIM ** -0.5
    outs = []
    for start in range(0, S, Q_BLOCK):
        end = start + Q_BLOCK
        qb = q[:, :, start:end]
        kb = k[:, :, :end]
        vb = v[:, :, :end]
        z = jnp.einsum('bhqd,bhkd->bhqk', qb, kb).astype(jnp.float32) * scale
        q_pos = start + jnp.arange(Q_BLOCK)[:, None]
        k_pos = jnp.arange(end)[None, :]
        mask = k_pos < q_pos
        log_keep = jnp.where(mask, jax.nn.log_sigmoid(-z), 0.0)
        after = lax.cumsum(log_keep, axis=3, reverse=True) - log_keep
        w = jnp.where(mask, jnp.exp(jax.nn.log_sigmoid(z) + after), 0.0)
        outs.append(jnp.einsum('bhqk,bhkd->bhqd', w.astype(vb.dtype), vb))
    o = jnp.concatenate(outs, axis=2).transpose(0, 2, 1, 3).reshape(B, S, D_MODEL)
    return o @ w_o


def causal_depthwise_conv(x, w, b):
    S = x.shape[1]
    xp = jnp.pad(x, ((0, 0), (RG_CONV - 1, 0), (0, 0)))
    y = b
    for tap in range(RG_CONV):
        y = y + xp[:, tap:tap + S] * w[tap]
    return y


def lru_combine(e1, e2):
    a1, b1 = e1
    a2, b2 = e2
    return a1 * a2, a2 * b1 + b2


def rglru_block(h, w_in, conv_w, conv_b, w_a, b_a, w_x, b_x, lam, w_o):
    B, S, _ = h.shape
    proj = h @ w_in
    gate = jax.nn.gelu(proj[..., :RG_WIDTH])
    xc = causal_depthwise_conv(proj[..., RG_WIDTH:], conv_w, conv_b)
    xh = xc.reshape(B, S, RG_BLOCKS, RG_BLOCK_DIM)
    r = jax.nn.sigmoid(jnp.einsum('bshi,hij->bshj', xh, w_a).reshape(B, S, RG_WIDTH) + b_a)
    i = jax.nn.sigmoid(jnp.einsum('bshi,hij->bshj', xh, w_x).reshape(B, S, RG_WIDTH) + b_x)
    log_a = (-RG_C * r.astype(jnp.float32)) * jax.nn.softplus(-lam.astype(jnp.float32))
    a = jnp.exp(log_a)
    mult = jnp.sqrt(-jnp.expm1(2.0 * log_a))
    bterm = mult * (i * xc).astype(jnp.float32)
    _, hs = lax.associative_scan(lru_combine, (a, bterm), axis=1)
    y = hs.astype(h.dtype) * gate
    return y @ w_o


def squared_relu_mlp(h, w1, w2):
    u = jax.nn.relu(h @ w1)
    return (u * u) @ w2


def setup_inputs(seed: int = 0) -> dict:
    key = jax.random.key(seed)
    ks = jax.random.split(key, 20)
    f32 = jnp.float32
    res_scale = (2.0 * DEPTH) ** -0.5
    x = jax.random.normal(ks[0], (BATCH, SEQ, D_MODEL), f32)
    norm_mix = 1.0 + 0.02 * jax.random.normal(ks[1], (DEPTH, D_MODEL), f32)
    norm_mlp = 1.0 + 0.02 * jax.random.normal(ks[2], (DEPTH, D_MODEL), f32)
    mlp_w1 = jax.random.normal(ks[3], (DEPTH, D_MODEL, D_FF), f32) * D_MODEL ** -0.5
    mlp_w2 = jax.random.normal(ks[4], (DEPTH, D_FF, D_MODEL), f32) * (D_FF ** -0.5 * res_scale)
    sb_w_qkv = jax.random.normal(ks[5], (N_SB_LAYERS, D_MODEL, 3 * D_MODEL), f32) * D_MODEL ** -0.5
    sb_w_o = jax.random.normal(ks[6], (N_SB_LAYERS, D_MODEL, D_MODEL), f32) * (D_MODEL ** -0.5 * res_scale)
    rg_w_in = jax.random.normal(ks[7], (N_RG_LAYERS, D_MODEL, 2 * RG_WIDTH), f32) * D_MODEL ** -0.5
    rg_conv_w = jax.random.normal(ks[8], (N_RG_LAYERS, RG_CONV, RG_WIDTH), f32) * RG_CONV ** -0.5
    rg_conv_b = 0.01 * jax.random.normal(ks[9], (N_RG_LAYERS, RG_WIDTH), f32)
    rg_w_a = jax.random.normal(ks[10], (N_RG_LAYERS, RG_BLOCKS, RG_BLOCK_DIM, RG_BLOCK_DIM), f32) * RG_BLOCK_DIM ** -0.5
    rg_b_a = 0.01 * jax.random.normal(ks[11], (N_RG_LAYERS, RG_WIDTH), f32)
    rg_w_x = jax.random.normal(ks[12], (N_RG_LAYERS, RG_BLOCKS, RG_BLOCK_DIM, RG_BLOCK_DIM), f32) * RG_BLOCK_DIM ** -0.5
    rg_b_x = 0.01 * jax.random.normal(ks[13], (N_RG_LAYERS, RG_WIDTH), f32)
    u = jax.random.uniform(ks[14], (N_RG_LAYERS, RG_WIDTH), f32, 0.9, 0.999)
    s = u ** (1.0 / RG_C)
    rg_lambda = jnp.log(s) - jnp.log1p(-s)
    rg_w_o = jax.random.normal(ks[15], (N_RG_LAYERS, RG_WIDTH, D_MODEL), f32) * (RG_WIDTH ** -0.5 * res_scale)
    norm_final = 1.0 + 0.02 * jax.random.normal(ks[16], (D_MODEL,), f32)
    return {'x': x, 'norm_mix': norm_mix, 'norm_mlp': norm_mlp, 'mlp_w1': mlp_w1, 'mlp_w2': mlp_w2,
            'sb_w_qkv': sb_w_qkv, 'sb_w_o': sb_w_o, 'rg_w_in': rg_w_in, 'rg_conv_w': rg_conv_w,
            'rg_conv_b': rg_conv_b, 'rg_w_a': rg_w_a, 'rg_b_a': rg_b_a, 'rg_w_x': rg_w_x,
            'rg_b_x': rg_b_x, 'rg_lambda': rg_lambda, 'rg_w_o': rg_w_o, 'norm_final': norm_final}


def reference(x, norm_mix, norm_mlp, mlp_w1, mlp_w2, sb_w_qkv, sb_w_o, rg_w_in, rg_conv_w,
              rg_conv_b, rg_w_a, rg_b_a, rg_w_x, rg_b_x, rg_lambda, rg_w_o, norm_final):
    ia = 0
    ib = 0
    for layer in range(DEPTH):
        h = rms_norm(x, norm_mix[layer])
        if layer % N_MIXERS == 0:
            y = stick_breaking_attention(h, sb_w_qkv[ia], sb_w_o[ia])
            ia += 1
        else:
            y = rglru_block(h, rg_w_in[ib], rg_conv_w[ib], rg_conv_b[ib], rg_w_a[ib], rg_b_a[ib],
                            rg_w_x[ib], rg_b_x[ib], rg_lambda[ib], rg_w_o[ib])
            ib += 1
        x = x + y
        h = rms_norm(x, norm_mlp[layer])
        x = x + squared_relu_mlp(h, mlp_w1[layer], mlp_w2[layer])
    return rms_norm(x, norm_final)
```

```python
import functools
import math

import jax
import jax.numpy as jnp
from jax import lax
from jax.experimental import pallas as pl
from jax.experimental.pallas import tpu as pltpu

EPS = 1e-6
SB_HEADS = 16
RG_BLOCKS = 16
RG_CONV = 4
RG_C = 8.0
GATE_GROUP = 256
HEAD_PAIR = 128
VMEM_LIMIT = 56 * 1024 * 1024

_BF16 = jnp.bfloat16
_F32 = jnp.float32


def _rms_norm(x, g):
    ms = jnp.mean(x * x, axis=-1, keepdims=True)
    return x * lax.rsqrt(ms + EPS) * g


def _softplus(z):
    return jnp.maximum(z, 0.0) + jnp.log(1.0 + jnp.exp(-jnp.abs(z)))


def _sigmoid(z):
    return 1.0 / (1.0 + jnp.exp(-z))


def _resident(shape):
    n = len(shape)
    return pl.BlockSpec(shape, lambda *_: (0,) * n, pipeline_mode=pl.Buffered(1))


def _norm_proj_kernel(x_ref, g_ref, w_ref, o_ref, h_ref, *, n_chunk):
    h_ref[...] = _rms_norm(x_ref[...], g_ref[...]).astype(_BF16)
    n = w_ref.shape[1]
    for c in range(0, n, n_chunk):
        o_ref[:, c:c + n_chunk] = jnp.dot(
            h_ref[...], w_ref[:, c:c + n_chunk], preferred_element_type=_F32).astype(o_ref.dtype)


def norm_proj(x2, g, w, *, tm=512, n_chunk=512):
    rows, d = x2.shape
    n = w.shape[1]
    tm = min(tm, rows)
    n_chunk = min(n_chunk, n)
    return pl.pallas_call(
        functools.partial(_norm_proj_kernel, n_chunk=n_chunk),
        grid=(rows // tm,),
        in_specs=[pl.BlockSpec((tm, d), lambda i: (i, 0)),
                  _resident((1, d)),
                  _resident((d, n))],
        out_specs=pl.BlockSpec((tm, n), lambda i: (i, 0)),
        out_shape=jax.ShapeDtypeStruct((rows, n), _BF16),
        scratch_shapes=[pltpu.VMEM((tm, d), _BF16)],
        compiler_params=pltpu.CompilerParams(dimension_semantics=("parallel",),
                                             vmem_limit_bytes=VMEM_LIMIT),
        name="norm_proj",
    )(x2, g.reshape(1, d), w)


def _sb_tile(q_h, k_t, v_t, tri, mask):
    z = lax.dot_general(q_h, k_t, (((1,), (1,)), ((), ())), preferred_element_type=_F32)
    sp = _softplus(z)
    if mask is not None:
        sp = jnp.where(mask, sp, 0.0)
    hi = sp.astype(_BF16)
    lo = (sp - hi.astype(_F32)).astype(_BF16)
    incl = (jnp.dot(hi, tri, preferred_element_type=_F32)
            + jnp.dot(lo, tri, preferred_element_type=_F32))
    p = jnp.exp(z + incl)
    if mask is not None:
        p = jnp.where(mask, p, 0.0)
    pv = jnp.dot(p.astype(_BF16), v_t, preferred_element_type=_F32)
    return pv, incl[:, 0:1]


def _sb_attn_kernel(q_ref, k_ref, v_ref, tri_ref, o_ref, acc_ref, run_ref, *, tq):
    i = pl.program_id(2)
    q = q_ref[0]
    lane = lax.broadcasted_iota(jnp.int32, q.shape, 1)
    first = lane < (HEAD_PAIR // 2)
    zero = jnp.zeros_like(q)
    q_heads = (jnp.where(first, q, zero), jnp.where(first, zero, q))
    tri = tri_ref[...]

    row = lax.broadcasted_iota(jnp.int32, (tq, tq), 0)
    col = lax.broadcasted_iota(jnp.int32, (tq, tq), 1)
    causal = col < row

    start = pl.multiple_of(i * tq, tq)
    k_t = k_ref[0, pl.ds(start, tq), :]
    v_t = v_ref[0, pl.ds(start, tq), :]
    for h in range(2):
        pv, tot = _sb_tile(q_heads[h], k_t, v_t, tri, causal)
        acc_ref[h] = pv
        run_ref[h] = tot

    def body(t, carry):
        start = pl.multiple_of((i - 1 - t) * tq, tq)
        k_t = k_ref[0, pl.ds(start, tq), :]
        v_t = v_ref[0, pl.ds(start, tq), :]
        for h in range(2):
            pv, tot = _sb_tile(q_heads[h], k_t, v_t, tri, None)
            run = run_ref[h]
            acc_ref[h] += jnp.exp(run) * pv
            run_ref[h] = run + tot
        return carry

    lax.fori_loop(0, i, body, 0)
    o_ref[0] = jnp.where(first, acc_ref[0], acc_ref[1]).astype(o_ref.dtype)


def sb_attention(qkv, *, tq=256):
    b, s, d3 = qkv.shape
    d = d3 // 3
    n_pairs = d // HEAD_PAIR
    tq = min(tq, s)
    r = lax.broadcasted_iota(jnp.int32, (tq, tq), 0)
    c = lax.broadcasted_iota(jnp.int32, (tq, tq), 1)
    tri = jnp.where(r >= c, -1.0, 0.0).astype(_BF16)
    return pl.pallas_call(
        functools.partial(_sb_attn_kernel, tq=tq),
        grid=(b, n_pairs, s // tq),
        in_specs=[pl.BlockSpec((1, tq, HEAD_PAIR), lambda bi, p, i: (bi, i, p)),
                  pl.BlockSpec((1, s, HEAD_PAIR), lambda bi, p, i: (bi, 0, n_pairs + p)),
                  pl.BlockSpec((1, s, HEAD_PAIR), lambda bi, p, i: (bi, 0, 2 * n_pairs + p)),
                  _resident((tq, tq))],
        out_specs=pl.BlockSpec((1, tq, HEAD_PAIR), lambda bi, p, i: (bi, i, p)),
        out_shape=jax.ShapeDtypeStruct((b, s, d), _BF16),
        scratch_shapes=[pltpu.VMEM((2, tq, HEAD_PAIR), _F32),
                        pltpu.VMEM((2, tq, 1), _F32)],
        compiler_params=pltpu.CompilerParams(
            dimension_semantics=("parallel", "parallel", "arbitrary"),
            vmem_limit_bytes=VMEM_LIMIT),
        name="sb_attention",
    )(qkv, qkv, qkv, tri)


def _gelu_tanh(x):
    c = math.sqrt(2.0 / math.pi)
    return 0.5 * x * (1.0 + jnp.tanh(c * (x + 0.044715 * (x * x * x))))


def _one_minus_exp(y, u):
    d = u - 1.0
    return jnp.where(d == 0.0, -y, -d * y / jnp.log1p(d))


def _rglru_kernel(x_ref, g_ref, win_ref, cw_ref, cb_ref, wg_ref, ba_ref, bx_ref, lam_ref, y_ref,
                  xs_ref, a_ref, b_ref, h_ref, *, ts):
    w = y_ref.shape[2]

    @pl.when(pl.program_id(1) == 0)
    def _():
        xs_ref[0:8, :] = jnp.zeros((8, w), _F32)
        h_ref[...] = jnp.zeros_like(h_ref)

    hn = _rms_norm(x_ref[0], g_ref[...]).astype(_BF16)
    gate = _gelu_tanh(jnp.dot(hn, win_ref[:, 0:w], preferred_element_type=_F32))
    xs_ref[8:8 + ts, :] = jnp.dot(hn, win_ref[:, w:2 * w], preferred_element_type=_F32)

    xc = cb_ref[...]
    for tap in range(RG_CONV):
        off = 8 - (RG_CONV - 1) + tap
        xc = xc + xs_ref[off:off + ts, :] * cw_ref[tap:tap + 1, :]
    xs_ref[0:8, :] = xs_ref[ts:ts + 8, :]

    neg_c_softplus = -RG_C * _softplus(-lam_ref[...])
    xcb = xc.astype(_BF16)
    for gi in range(w // GATE_GROUP):
        sl = slice(gi * GATE_GROUP, (gi + 1) * GATE_GROUP)
        pre = jnp.dot(xcb[:, sl], wg_ref[gi], preferred_element_type=_F32)
        r = _sigmoid(pre[:, 0:GATE_GROUP] + ba_ref[:, sl])
        ig = _sigmoid(pre[:, GATE_GROUP:2 * GATE_GROUP] + bx_ref[:, sl])
        log_a = r * neg_c_softplus[:, sl]
        a = jnp.exp(log_a)
        a_ref[:, sl] = a
        b_ref[:, sl] = jnp.sqrt(_one_minus_exp(2.0 * log_a, a * a)) * (ig * xc[:, sl])

    a = a_ref[...]
    bt = b_ref[...]
    sub = lax.broadcasted_iota(jnp.int32, (ts, w), 0) % 8
    for k in (1, 2, 4):
        keep = sub >= k
        a_prev = jnp.where(keep, pltpu.roll(a, k, 0), 1.0)
        b_prev = jnp.where(keep, pltpu.roll(bt, k, 0), 0.0)
        bt = a * b_prev + bt
        a = a * a_prev
    a_ref[...] = a
    b_ref[...] = bt

    carry = h_ref[...]
    for gidx in range(ts // 8):
        rows = slice(gidx * 8, gidx * 8 + 8)
        hs = a_ref[rows, :] * carry + b_ref[rows, :]
        b_ref[rows, :] = hs
        carry = hs[7:8, :]
    h_ref[...] = carry

    y_ref[0] = (b_ref[...] * gate).astype(y_ref.dtype)


def rglru_mixer(x, g, w_in, conv_w, conv_b, w_gates, b_a, b_x, lam, *, ts=256):
    b, s, d = x.shape
    w = w_in.shape[1] // 2
    ts = min(ts, s)
    row = lambda v: v.reshape(1, -1)
    return pl.pallas_call(
        functools.partial(_rglru_kernel, ts=ts),
        grid=(b, s // ts),
        in_specs=[pl.BlockSpec((1, ts, d), lambda bi, i: (bi, i, 0)),
                  _resident((1, d)),
                  _resident((d, 2 * w)),
                  _resident((RG_CONV, w)),
                  _resident((1, w)),
                  _resident(w_gates.shape),
                  _resident((1, w)),
                  _resident((1, w)),
                  _resident((1, w))],
        out_specs=pl.BlockSpec((1, ts, w), lambda bi, i: (bi, i, 0)),
        out_shape=jax.ShapeDtypeStruct((b, s, w), _BF16),
        scratch_shapes=[pltpu.VMEM((ts + 8, w), _F32),
                        pltpu.VMEM((ts, w), _F32),
                        pltpu.VMEM((ts, w), _F32),
                        pltpu.VMEM((1, w), _F32)],
        compiler_params=pltpu.CompilerParams(dimension_semantics=("parallel", "arbitrary"),
                                             vmem_limit_bytes=VMEM_LIMIT),
        name="rglru_mixer",
    )(x, row(g), w_in, conv_w, row(conv_b), w_gates, row(b_a), row(b_x), row(lam))


def _proj_mlp_kernel(x_ref, y_ref, wo_ref, g_ref, w1_ref, w2_ref, gf_ref, o_ref, x1_ref, h_ref,
                     *, ff_chunk, final_norm):
    x1_ref[...] = x_ref[...] + jnp.dot(y_ref[...], wo_ref[...], preferred_element_type=_F32)
    h_ref[...] = _rms_norm(x1_ref[...], g_ref[...]).astype(_BF16)
    d_ff = w1_ref.shape[1]
    for c in range(0, d_ff, ff_chunk):
        u = jnp.maximum(jnp.dot(h_ref[...], w1_ref[:, c:c + ff_chunk],
                                preferred_element_type=_F32), 0.0)
        x1_ref[...] += jnp.dot((u * u).astype(_BF16), w2_ref[c:c + ff_chunk, :],
                               preferred_element_type=_F32)
    out = x1_ref[...]
    if final_norm:
        out = _rms_norm(out, gf_ref[...])
    o_ref[...] = out


def proj_mlp(x2, y2, w_o, g, w1, w2, g_final, *, final_norm, tm=512, ff_chunk=1024):
    rows, d = x2.shape
    d_ff = w1.shape[1]
    tm = min(tm, rows)
    ff_chunk = min(ff_chunk, d_ff)
    return pl.pallas_call(
        functools.partial(_proj_mlp_kernel, ff_chunk=ff_chunk, final_norm=final_norm),
        grid=(rows // tm,),
        in_specs=[pl.BlockSpec((tm, d), lambda i: (i, 0)),
                  pl.BlockSpec((tm, y2.shape[1]), lambda i: (i, 0)),
                  _resident(w_o.shape),
                  _resident((1, d)),
                  _resident((d, d_ff)),
                  _resident((d_ff, d)),
                  _resident((1, d))],
        out_specs=pl.BlockSpec((tm, d), lambda i: (i, 0)),
        out_shape=jax.ShapeDtypeStruct((rows, d), _F32),
        scratch_shapes=[pltpu.VMEM((tm, d), _F32),
                        pltpu.VMEM((tm, d), _BF16)],
        compiler_params=pltpu.CompilerParams(dimension_semantics=("parallel",),
                                             vmem_limit_bytes=VMEM_LIMIT),
        name="proj_mlp",
    )(x2, y2, w_o, g.reshape(1, d), w1, w2, g_final.reshape(1, d))


def _block_diag_groups(w_blocks):
    nb, bd, _ = w_blocks.shape
    per = GATE_GROUP // bd
    wg = w_blocks.reshape(nb // per, per, bd, bd)
    eye = jnp.eye(per, dtype=w_blocks.dtype)
    return jnp.einsum("girc,ij->girjc", wg, eye).reshape(nb // per, GATE_GROUP, GATE_GROUP)


def kernel(x, norm_mix, norm_mlp, mlp_w1, mlp_w2, sb_w_qkv, sb_w_o, rg_w_in, rg_conv_w, rg_conv_b,
           rg_w_a, rg_b_a, rg_w_x, rg_b_x, rg_lambda, rg_w_o, norm_final):
    b, s, d = x.shape
    depth = norm_mix.shape[0]
    head_dim = d // SB_HEADS
    q_scale = jnp.concatenate([jnp.full((d,), head_dim ** -0.5, _F32), jnp.ones((2 * d,), _F32)])

    x2 = x.reshape(b * s, d)
    ia = ib = 0
    for layer in range(depth):
        if layer % 2 == 0:
            w_qkv = (sb_w_qkv[ia] * q_scale).astype(_BF16)
            qkv = norm_proj(x2, norm_mix[layer], w_qkv)
            y = sb_attention(qkv.reshape(b, s, 3 * d))
            w_o = sb_w_o[ia]
            ia += 1
        else:
            w_gates = jnp.concatenate([_block_diag_groups(rg_w_a[ib]),
                                       _block_diag_groups(rg_w_x[ib])], axis=-1).astype(_BF16)
            y = rglru_mixer(x2.reshape(b, s, d), norm_mix[layer], rg_w_in[ib].astype(_BF16),
                            rg_conv_w[ib], rg_conv_b[ib], w_gates, rg_b_a[ib], rg_b_x[ib],
                            rg_lambda[ib])
            w_o = rg_w_o[ib]
            ib += 1
        x2 = proj_mlp(x2, y.reshape(b * s, d), w_o.astype(_BF16), norm_mlp[layer],
                      mlp_w1[layer].astype(_BF16), mlp_w2[layer].astype(_BF16), norm_final,
                      final_norm=(layer == depth - 1))
    return x2.reshape(b, s, d)
```

```python
import functools
import math

import jax
import jax.numpy as jnp
from jax import lax
from jax.experimental import pallas as pl
from jax.experimental.pallas import tpu as pltpu

EPS = 1e-6
SB_HEADS = 16
RG_BLOCKS = 16
RG_CONV = 4
RG_C = 8.0
GATE_GROUP = 256
HEAD_PAIR = 128
VMEM_LIMIT = 56 * 1024 * 1024

_BF16 = jnp.bfloat16
_F32 = jnp.float32


def _rms_norm(x, g):
    ms = jnp.mean(x * x, axis=-1, keepdims=True)
    return x * lax.rsqrt(ms + EPS) * g


def _softplus(z):
    return jnp.maximum(z, 0.0) + jnp.log(1.0 + jnp.exp(-jnp.abs(z)))


def _sigmoid(z):
    return 1.0 / (1.0 + jnp.exp(-z))


def _resident(shape):
    n = len(shape)
    return pl.BlockSpec(shape, lambda *_: (0,) * n, pipeline_mode=pl.Buffered(1))


def _norm_proj_kernel(x_ref, g_ref, w_ref, o_ref, h_ref, *, n_chunk):
    h_ref[...] = _rms_norm(x_ref[...], g_ref[...]).astype(_BF16)
    n = w_ref.shape[1]
    for c in range(0, n, n_chunk):
        o_ref[:, c:c + n_chunk] = jnp.dot(
            h_ref[...], w_ref[:, c:c + n_chunk], preferred_element_type=_F32).astype(o_ref.dtype)


def norm_proj(x2, g, w, *, tm=512, n_chunk=512):
    rows, d = x2.shape
    n = w.shape[1]
    tm = min(tm, rows)
    n_chunk = min(n_chunk, n)
    return pl.pallas_call(
        functools.partial(_norm_proj_kernel, n_chunk=n_chunk),
        grid=(rows // tm,),
        in_specs=[pl.BlockSpec((tm, d), lambda i: (i, 0)),
                  _resident((1, d)),
                  _resident((d, n))],
        out_specs=pl.BlockSpec((tm, n), lambda i: (i, 0)),
        out_shape=jax.ShapeDtypeStruct((rows, n), _BF16),
        scratch_shapes=[pltpu.VMEM((tm, d), _BF16)],
        compiler_params=pltpu.CompilerParams(dimension_semantics=("parallel",),
                                             vmem_limit_bytes=VMEM_LIMIT),
        name="norm_proj",
    )(x2, g.reshape(1, d), w)


_SIGN_BIT = 0x80000000


def _softplus2(z):
    neg_abs = lax.bitcast_convert_type(
        lax.bitcast_convert_type(z, jnp.uint32) | jnp.uint32(_SIGN_BIT), _F32)
    return jnp.maximum(z, 0.0) + jnp.log2(1.0 + jnp.exp2(neg_abs))


_EXP2_ZERO_BELOW = -152.0


def _sb_scores(chains, tri):
    zs = [lax.dot_general(q_h, k_t, (((1,), (1,)), ((), ())), preferred_element_type=_F32)
          for q_h, k_t, _ in chains]
    hilos = []
    for z, (_, _, mask) in zip(zs, chains):
        sp = _softplus2(z)
        if mask is not None:
            sp = jnp.where(mask, sp, 0.0)
        hi = sp.astype(_BF16)
        lo = (sp - hi.astype(_F32)).astype(_BF16)
        hilos.append(jnp.concatenate([hi, lo], axis=1))
    incls = [jnp.dot(hilo, tri, preferred_element_type=_F32) for hilo in hilos]
    return zs, incls


def _sb_values(zs, incls, v_heads, runs, mask):
    ps, new_runs = [], []
    for h, (z, incl) in enumerate(zip(zs, incls)):
        e = z + incl
        tot = incl[:, 0:1]
        if runs is not None:
            e = e + runs[h]
            tot = tot + runs[h]
        p = jnp.exp2(e)
        if mask is not None:
            p = jnp.where(mask, p, 0.0)
        ps.append(p.astype(_BF16))
        new_runs.append(tot)
    pv = jnp.dot(jnp.concatenate(ps, axis=1), v_heads, preferred_element_type=_F32)
    return pv, new_runs


def _sb_attn_kernel(q_ref, k_ref, v_ref, tri_ref, o_ref, acc_ref, run_ref, *, tq, head_dim):
    i = pl.program_id(2)
    q = q_ref[0]
    n_heads = q.shape[1] // head_dim
    head_of_lane = lax.broadcasted_iota(jnp.int32, q.shape, 1) // head_dim
    zero = jnp.zeros_like(q)
    q_heads = [jnp.where(head_of_lane == h, q, zero) for h in range(n_heads)]
    tri = tri_ref[...]
    row = lax.broadcasted_iota(jnp.int32, (tq, tq), 0)
    col = lax.broadcasted_iota(jnp.int32, (tq, tq), 1)
    causal = col < row

    def k_tile(j):
        return k_ref[0, pl.ds(pl.multiple_of(j * tq, tq), tq), :]

    def v_tile(j):
        v_t = v_ref[0, pl.ds(pl.multiple_of(j * tq, tq), tq), :]
        return jnp.concatenate([jnp.where(head_of_lane == h, v_t, zero)
                                for h in range(n_heads)], axis=0)

    @pl.when(i == 0)
    def _():
        k_t = k_tile(0)
        zs, incls = _sb_scores([(q_h, k_t, causal) for q_h in q_heads], tri)
        pv, _ = _sb_values(zs, incls, v_tile(0), None, causal)
        o_ref[0] = pv.astype(o_ref.dtype)

    @pl.when(i > 0)
    def _():
        k_d, k_p = k_tile(i), k_tile(i - 1)
        zs, incls = _sb_scores([(q_h, k_d, causal) for q_h in q_heads]
                               + [(q_h, k_p, None) for q_h in q_heads], tri)
        pv_d, runs = _sb_values(zs[:n_heads], incls[:n_heads], v_tile(i), None, causal)
        pv_p, runs = _sb_values(zs[n_heads:], incls[n_heads:], v_tile(i - 1), runs, None)
        acc_ref[...] = pv_d + pv_p

        def store_runs(runs):
            top = runs[0]
            for h in range(n_heads):
                run_ref[h] = runs[h]
                top = jnp.maximum(top, runs[h])
            return jnp.max(top)

        def cond(carry):
            j, top_run = carry
            return jnp.logical_and(j >= 0, top_run >= _EXP2_ZERO_BELOW)

        def body(carry):
            j, _ = carry
            k_t = k_tile(j)
            zs, incls = _sb_scores([(q_h, k_t, None) for q_h in q_heads], tri)
            pv, runs = _sb_values(zs, incls, v_tile(j), [run_ref[h] for h in range(n_heads)],
                                  None)
            acc_ref[...] += pv
            return j - 1, store_runs(runs)

        lax.while_loop(cond, body, (i - 2, store_runs(runs)))
        o_ref[0] = acc_ref[...].astype(o_ref.dtype)


def sb_attention(qkv, *, head_dim, tq=256, width=2 * HEAD_PAIR):
    b, s, d3 = qkv.shape
    d = d3 // 3
    n_blk = d // width
    tq = min(tq, s)
    r = lax.broadcasted_iota(jnp.int32, (2 * tq, tq), 0) % tq
    c = lax.broadcasted_iota(jnp.int32, (2 * tq, tq), 1)
    tri = jnp.where(r >= c, -1.0, 0.0).astype(_BF16)
    return pl.pallas_call(
        functools.partial(_sb_attn_kernel, tq=tq, head_dim=head_dim),
        grid=(b, n_blk, s // tq),
        in_specs=[pl.BlockSpec((1, tq, width), lambda bi, p, i: (bi, i, p)),
                  pl.BlockSpec((1, s, width), lambda bi, p, i: (bi, 0, n_blk + p)),
                  pl.BlockSpec((1, s, width), lambda bi, p, i: (bi, 0, 2 * n_blk + p)),
                  _resident((2 * tq, tq))],
        out_specs=pl.BlockSpec((1, tq, width), lambda bi, p, i: (bi, i, p)),
        out_shape=jax.ShapeDtypeStruct((b, s, d), _BF16),
        scratch_shapes=[pltpu.VMEM((tq, width), _F32),
                        pltpu.VMEM((width // head_dim, tq, 1), _F32)],
        compiler_params=pltpu.CompilerParams(
            dimension_semantics=("parallel", "parallel", "arbitrary"),
            vmem_limit_bytes=VMEM_LIMIT),
        name="sb_attention",
    )(qkv, qkv, qkv, tri)


def _gelu_tanh(x):
    c = math.sqrt(2.0 / math.pi)
    return 0.5 * x * (1.0 + jnp.tanh(c * (x + 0.044715 * (x * x * x))))


def _one_minus_exp(y, u):
    d = u - 1.0
    return jnp.where(d == 0.0, -y, -d * y / jnp.log1p(d))


def _rglru_kernel(x_ref, g_ref, win_ref, cw_ref, cb_ref, wg_ref, ba_ref, bx_ref, lam_ref, y_ref,
                  xs_ref, a_ref, b_ref, h_ref, *, ts):
    w = y_ref.shape[2]

    @pl.when(pl.program_id(1) == 0)
    def _():
        xs_ref[0:8, :] = jnp.zeros((8, w), _F32)
        h_ref[...] = jnp.zeros_like(h_ref)

    hn = _rms_norm(x_ref[0], g_ref[...]).astype(_BF16)
    gate = _gelu_tanh(jnp.dot(hn, win_ref[:, 0:w], preferred_element_type=_F32))
    xs_ref[8:8 + ts, :] = jnp.dot(hn, win_ref[:, w:2 * w], preferred_element_type=_F32)

    xc = cb_ref[...]
    for tap in range(RG_CONV):
        off = 8 - (RG_CONV - 1) + tap
        xc = xc + xs_ref[off:off + ts, :] * cw_ref[tap:tap + 1, :]
    xs_ref[0:8, :] = xs_ref[ts:ts + 8, :]

    neg_c_softplus = -RG_C * _softplus(-lam_ref[...])
    xcb = xc.astype(_BF16)
    for gi in range(w // GATE_GROUP):
        sl = slice(gi * GATE_GROUP, (gi + 1) * GATE_GROUP)
        pre = jnp.dot(xcb[:, sl], wg_ref[gi], preferred_element_type=_F32)
        r = _sigmoid(pre[:, 0:GATE_GROUP] + ba_ref[:, sl])
        ig = _sigmoid(pre[:, GATE_GROUP:2 * GATE_GROUP] + bx_ref[:, sl])
        log_a = r * neg_c_softplus[:, sl]
        a = jnp.exp(log_a)
        a_ref[:, sl] = a
        b_ref[:, sl] = jnp.sqrt(_one_minus_exp(2.0 * log_a, a * a)) * (ig * xc[:, sl])

    a = a_ref[...]
    bt = b_ref[...]
    sub = lax.broadcasted_iota(jnp.int32, (ts, w), 0) % 8
    for k in (1, 2, 4):
        keep = sub >= k
        a_prev = jnp.where(keep, pltpu.roll(a, k, 0), 1.0)
        b_prev = jnp.where(keep, pltpu.roll(bt, k, 0), 0.0)
        bt = a * b_prev + bt
        a = a * a_prev
    a_ref[...] = a
    b_ref[...] = bt

    carry = h_ref[...]
    for gidx in range(ts // 8):
        rows = slice(gidx * 8, gidx * 8 + 8)
        hs = a_ref[rows, :] * carry + b_ref[rows, :]
        b_ref[rows, :] = hs
        carry = hs[7:8, :]
    h_ref[...] = carry

    y_ref[0] = (b_ref[...] * gate).astype(y_ref.dtype)


def rglru_mixer(x, g, w_in, conv_w, conv_b, w_gates, b_a, b_x, lam, *, ts=256):
    b, s, d = x.shape
    w = w_in.shape[1] // 2
    ts = min(ts, s)
    row = lambda v: v.reshape(1, -1)
    return pl.pallas_call(
        functools.partial(_rglru_kernel, ts=ts),
        grid=(b, s // ts),
        in_specs=[pl.BlockSpec((1, ts, d), lambda bi, i: (bi, i, 0)),
                  _resident((1, d)),
                  _resident((d, 2 * w)),
                  _resident((RG_CONV, w)),
                  _resident((1, w)),
                  _resident(w_gates.shape),
                  _resident((1, w)),
                  _resident((1, w)),
                  _resident((1, w))],
        out_specs=pl.BlockSpec((1, ts, w), lambda bi, i: (bi, i, 0)),
        out_shape=jax.ShapeDtypeStruct((b, s, w), _BF16),
        scratch_shapes=[pltpu.VMEM((ts + 8, w), _F32),
                        pltpu.VMEM((ts, w), _F32),
                        pltpu.VMEM((ts, w), _F32),
                        pltpu.VMEM((1, w), _F32)],
        compiler_params=pltpu.CompilerParams(dimension_semantics=("parallel", "arbitrary"),
                                             vmem_limit_bytes=VMEM_LIMIT),
        name="rglru_mixer",
    )(x, row(g), w_in, conv_w, row(conv_b), w_gates, row(b_a), row(b_x), row(lam))


def _proj_mlp_kernel(x_ref, y_ref, wo_ref, g_ref, w1_ref, w2_ref, gf_ref, o_ref, x1_ref, h_ref,
                     *, ff_chunk, final_norm):
    x1_ref[...] = x_ref[...] + jnp.dot(y_ref[...], wo_ref[...], preferred_element_type=_F32)
    h_ref[...] = _rms_norm(x1_ref[...], g_ref[...]).astype(_BF16)
    d_ff = w1_ref.shape[1]
    for c in range(0, d_ff, ff_chunk):
        u = jnp.maximum(jnp.dot(h_ref[...], w1_ref[:, c:c + ff_chunk],
                                preferred_element_type=_F32), 0.0)
        x1_ref[...] += jnp.dot((u * u).astype(_BF16), w2_ref[c:c + ff_chunk, :],
                               preferred_element_type=_F32)
    out = x1_ref[...]
    if final_norm:
        out = _rms_norm(out, gf_ref[...])
    o_ref[...] = out


def proj_mlp(x2, y2, w_o, g, w1, w2, g_final, *, final_norm, tm=512, ff_chunk=1024):
    rows, d = x2.shape
    d_ff = w1.shape[1]
    tm = min(tm, rows)
    ff_chunk = min(ff_chunk, d_ff)
    return pl.pallas_call(
        functools.partial(_proj_mlp_kernel, ff_chunk=ff_chunk, final_norm=final_norm),
        grid=(rows // tm,),
        in_specs=[pl.BlockSpec((tm, d), lambda i: (i, 0)),
                  pl.BlockSpec((tm, y2.shape[1]), lambda i: (i, 0)),
                  _resident(w_o.shape),
                  _resident((1, d)),
                  _resident((d, d_ff)),
                  _resident((d_ff, d)),
                  _resident((1, d))],
        out_specs=pl.BlockSpec((tm, d), lambda i: (i, 0)),
        out_shape=jax.ShapeDtypeStruct((rows, d), _F32),
        scratch_shapes=[pltpu.VMEM((tm, d), _F32),
                        pltpu.VMEM((tm, d), _BF16)],
        compiler_params=pltpu.CompilerParams(dimension_semantics=("parallel",),
                                             vmem_limit_bytes=VMEM_LIMIT),
        name="proj_mlp",
    )(x2, y2, w_o, g.reshape(1, d), w1, w2, g_final.reshape(1, d))


def _block_diag_groups(w_blocks):
    nb, bd, _ = w_blocks.shape
    per = GATE_GROUP // bd
    wg = w_blocks.reshape(nb // per, per, bd, bd)
    eye = jnp.eye(per, dtype=w_blocks.dtype)
    return jnp.einsum("girc,ij->girjc", wg, eye).reshape(nb // per, GATE_GROUP, GATE_GROUP)


def kernel(x, norm_mix, norm_mlp, mlp_w1, mlp_w2, sb_w_qkv, sb_w_o, rg_w_in, rg_conv_w, rg_conv_b,
           rg_w_a, rg_b_a, rg_w_x, rg_b_x, rg_lambda, rg_w_o, norm_final):
    b, s, d = x.shape
    depth = norm_mix.shape[0]
    head_dim = d // SB_HEADS
    q_scale = jnp.concatenate([jnp.full((d,), math.log2(math.e) * head_dim ** -0.5, _F32),
                               jnp.ones((2 * d,), _F32)])

    x2 = x.reshape(b * s, d)
    ia = ib = 0
    for layer in range(depth):
        if layer % 2 == 0:
            w_qkv = (sb_w_qkv[ia] * q_scale).astype(_BF16)
            qkv = norm_proj(x2, norm_mix[layer], w_qkv)
            y = sb_attention(qkv.reshape(b, s, 3 * d), head_dim=head_dim)
            w_o = sb_w_o[ia]
            ia += 1
        else:
            w_gates = jnp.concatenate([_block_diag_groups(rg_w_a[ib]),
                                       _block_diag_groups(rg_w_x[ib])], axis=-1).astype(_BF16)
            y = rglru_mixer(x2.reshape(b, s, d), norm_mix[layer], rg_w_in[ib].astype(_BF16),
                            rg_conv_w[ib], rg_conv_b[ib], w_gates, rg_b_a[ib], rg_b_x[ib],
                            rg_lambda[ib])
            w_o = rg_w_o[ib]
            ib += 1
        x2 = proj_mlp(x2, y.reshape(b * s, d), w_o.astype(_BF16), norm_mlp[layer],
                      mlp_w1[layer].astype(_BF16), mlp_w2[layer].astype(_BF16), norm_final,
                      final_norm=(layer == depth - 1))
    return x2.reshape(b, s, d)
```

```python
import functools
import math

import jax
import jax.numpy as jnp
from jax import lax
from jax.experimental import pallas as pl
from jax.experimental.pallas import tpu as pltpu

EPS = 1e-6
SB_HEADS = 16
RG_CONV = 4
RG_C = 8.0
GATE_GROUP = 256
HEAD_BLOCK = 256
SUBLANES = 8
VMEM_LIMIT = 56 * 1024 * 1024

_BF16 = jnp.bfloat16
_F32 = jnp.float32


def _rms_norm(x, g):
    ms = jnp.mean(x * x, axis=-1, keepdims=True)
    return x * lax.rsqrt(ms + EPS) * g


def _softplus(z):
    return jnp.maximum(z, 0.0) + jnp.log(1.0 + jnp.exp(-jnp.abs(z)))


def _sigmoid(z):
    return 0.5 * jnp.tanh(0.5 * z) + 0.5


def _sqrt_nonneg(x):
    return jnp.where(x > 0.0, x * lax.rsqrt(x), 0.0)


def _resident(shape):
    n = len(shape)
    return pl.BlockSpec(shape, lambda *_: (0,) * n, pipeline_mode=pl.Buffered(1))


def _token_shape(layout, batch, seq, d):
    return (batch * seq, d) if layout == "bs" else (seq, batch * d)


def _token_spec(layout, tm, d, seq):
    if layout == "bs":
        return pl.BlockSpec((tm, d), lambda b, j: (b * (seq // tm) + j, 0))
    return pl.BlockSpec((tm, d), lambda b, j: (j, b))


def _norm_proj_kernel(x_ref, g_ref, w_ref, o_ref, h_ref, *, n_chunk):
    h_ref[...] = _rms_norm(x_ref[...], g_ref[...]).astype(_BF16)
    n = w_ref.shape[1]
    for c in range(0, n, n_chunk):
        o_ref[:, c:c + n_chunk] = jnp.dot(
            h_ref[...], w_ref[:, c:c + n_chunk], preferred_element_type=_F32).astype(o_ref.dtype)


def norm_proj(x, g, w, *, batch, seq, in_layout, tm=512, n_chunk=512):
    d, n = w.shape
    tm = min(tm, seq)
    n_chunk = min(n_chunk, n)
    return pl.pallas_call(
        functools.partial(_norm_proj_kernel, n_chunk=n_chunk),
        grid=(batch, seq // tm),
        in_specs=[_token_spec(in_layout, tm, d, seq),
                  _resident((1, d)),
                  _resident((d, n))],
        out_specs=_token_spec("sb", tm, n, seq),
        out_shape=jax.ShapeDtypeStruct(_token_shape("sb", batch, seq, n), _BF16),
        scratch_shapes=[pltpu.VMEM((tm, d), _BF16)],
        compiler_params=pltpu.CompilerParams(dimension_semantics=("parallel", "parallel"),
                                             vmem_limit_bytes=VMEM_LIMIT),
        name="norm_proj",
    )(x, g.reshape(1, d), w)


def _softplus2(z):
    return jnp.maximum(z, 0.0) + jnp.log2(1.0 + jnp.exp2(-jnp.abs(z)))


_EXP2_ZERO_BELOW = -152.0


def _sb_scores(chains, tri):
    zs = [lax.dot_general(q_h, k_t, (((1,), (1,)), ((), ())), preferred_element_type=_F32)
          for q_h, k_t, _ in chains]
    hilos = []
    for z, (_, _, mask) in zip(zs, chains):
        sp = _softplus2(z)
        if mask is not None:
            sp = jnp.where(mask, sp, 0.0)
        hi = sp.astype(_BF16)
        lo = (sp - hi.astype(_F32)).astype(_BF16)
        hilos.append(jnp.concatenate([hi, lo], axis=1))
    incls = [jnp.dot(hilo, tri, preferred_element_type=_F32) for hilo in hilos]
    return zs, incls


def _sb_values(zs, incls, v_heads, runs, mask):
    ps, new_runs = [], []
    for h, (z, incl) in enumerate(zip(zs, incls)):
        e = z + incl
        tot = incl[:, 0:1]
        if runs is not None:
            e = e + runs[h]
            tot = tot + runs[h]
        p = jnp.exp2(e)
        if mask is not None:
            p = jnp.where(mask, p, 0.0)
        ps.append(p.astype(_BF16))
        new_runs.append(tot)
    pv = jnp.dot(jnp.concatenate(ps, axis=1), v_heads, preferred_element_type=_F32)
    return pv, new_runs


def _sb_attn_kernel(q_ref, k_ref, v_ref, tri_ref, o_ref, acc_ref, run_ref, *, tq, head_dim):
    i = pl.program_id(2)
    q = q_ref[...]
    n_heads = q.shape[1] // head_dim
    head_of_lane = lax.broadcasted_iota(jnp.int32, q.shape, 1) // head_dim
    zero = jnp.zeros_like(q)
    q_heads = [jnp.where(head_of_lane == h, q, zero) for h in range(n_heads)]
    tri = tri_ref[...]
    row = lax.broadcasted_iota(jnp.int32, (tq, tq), 0)
    col = lax.broadcasted_iota(jnp.int32, (tq, tq), 1)
    causal = col < row

    def k_tile(j):
        return k_ref[pl.ds(pl.multiple_of(j * tq, tq), tq), :]

    def v_tile(j):
        v_t = v_ref[pl.ds(pl.multiple_of(j * tq, tq), tq), :]
        return jnp.concatenate([jnp.where(head_of_lane == h, v_t, zero)
                                for h in range(n_heads)], axis=0)

    @pl.when(i == 0)
    def _():
        k_t = k_tile(0)
        zs, incls = _sb_scores([(q_h, k_t, causal) for q_h in q_heads], tri)
        pv, _ = _sb_values(zs, incls, v_tile(0), None, causal)
        o_ref[...] = pv.astype(o_ref.dtype)

    @pl.when(i > 0)
    def _():
        k_d, k_p = k_tile(i), k_tile(i - 1)
        zs, incls = _sb_scores([(q_h, k_d, causal) for q_h in q_heads]
                               + [(q_h, k_p, None) for q_h in q_heads], tri)
        pv_d, runs = _sb_values(zs[:n_heads], incls[:n_heads], v_tile(i), None, causal)
        pv_p, runs = _sb_values(zs[n_heads:], incls[n_heads:], v_tile(i - 1), runs, None)
        acc_ref[...] = pv_d + pv_p

        def store_runs(runs):
            top = runs[0]
            for h in range(n_heads):
                run_ref[h] = runs[h]
                top = jnp.maximum(top, runs[h])
            return jnp.max(top)

        def cond(carry):
            j, top_run = carry
            return jnp.logical_and(j >= 0, top_run >= _EXP2_ZERO_BELOW)

        def body(carry):
            j, _ = carry
            k_t = k_tile(j)
            zs, incls = _sb_scores([(q_h, k_t, None) for q_h in q_heads], tri)
            pv, runs = _sb_values(zs, incls, v_tile(j), [run_ref[h] for h in range(n_heads)],
                                  None)
            acc_ref[...] += pv
            return j - 1, store_runs(runs)

        lax.while_loop(cond, body, (i - 2, store_runs(runs)))
        o_ref[...] = acc_ref[...].astype(o_ref.dtype)


def sb_attention(qkv, *, batch, head_dim, tq=256, width=HEAD_BLOCK):
    s = qkv.shape[0]
    d = qkv.shape[1] // (3 * batch)
    n_blk = d // width
    tq = min(tq, s)
    r = lax.broadcasted_iota(jnp.int32, (2 * tq, tq), 0) % tq
    c = lax.broadcasted_iota(jnp.int32, (2 * tq, tq), 1)
    tri = jnp.where(r >= c, -1.0, 0.0).astype(_BF16)
    return pl.pallas_call(
        functools.partial(_sb_attn_kernel, tq=tq, head_dim=head_dim),
        grid=(batch, n_blk, s // tq),
        in_specs=[pl.BlockSpec((tq, width), lambda b, p, i: (i, b * 3 * n_blk + p)),
                  pl.BlockSpec((s, width), lambda b, p, i: (0, b * 3 * n_blk + n_blk + p)),
                  pl.BlockSpec((s, width), lambda b, p, i: (0, b * 3 * n_blk + 2 * n_blk + p)),
                  _resident((2 * tq, tq))],
        out_specs=pl.BlockSpec((tq, width), lambda b, p, i: (i, b * n_blk + p)),
        out_shape=jax.ShapeDtypeStruct((s, batch * d), _BF16),
        scratch_shapes=[pltpu.VMEM((tq, width), _F32),
                        pltpu.VMEM((width // head_dim, tq, 1), _F32)],
        compiler_params=pltpu.CompilerParams(
            dimension_semantics=("parallel", "parallel", "arbitrary"),
            vmem_limit_bytes=VMEM_LIMIT),
        name="sb_attention",
    )(qkv, qkv, qkv, tri)


def _gelu_tanh(x):
    c = math.sqrt(2.0 / math.pi)
    return 0.5 * x * (1.0 + jnp.tanh(c * (x + 0.044715 * (x * x * x))))


def _one_minus_exp(y, u):
    d = u - 1.0
    return jnp.where(d == 0.0, -y, -d * y / jnp.log1p(d))


def _rglru_kernel(x_ref, g_ref, win_ref, cw_ref, cb_ref, wg_ref, ba_ref, bx_ref, lam_ref, y_ref,
                  xs_ref, a_ref, b_ref, h_ref, *, tc, nb):
    w = y_ref.shape[1]
    rows = tc * nb
    tail = (RG_CONV - 1) * nb

    @pl.when(pl.program_id(0) == 0)
    def _():
        xs_ref[0:tail, :] = jnp.zeros((tail, w), _F32)
        h_ref[...] = jnp.zeros_like(h_ref)

    hn = _rms_norm(x_ref[...], g_ref[...]).astype(_BF16)
    gate = _gelu_tanh(jnp.dot(hn, win_ref[:, 0:w], preferred_element_type=_F32))
    xs_ref[tail:tail + rows, :] = jnp.dot(hn, win_ref[:, w:2 * w], preferred_element_type=_F32)

    xc = cb_ref[...]
    for tap in range(RG_CONV):
        xc = xc + xs_ref[tap * nb:tap * nb + rows, :] * cw_ref[tap:tap + 1, :]
    xs_ref[0:tail, :] = xs_ref[rows:rows + tail, :]

    neg_c_softplus = -RG_C * _softplus(-lam_ref[...])
    xcb = xc.astype(_BF16)
    for gi in range(w // GATE_GROUP):
        sl = slice(gi * GATE_GROUP, (gi + 1) * GATE_GROUP)
        pre = jnp.dot(xcb[:, sl], wg_ref[gi], preferred_element_type=_F32)
        r = _sigmoid(pre[:, 0:GATE_GROUP] + ba_ref[:, sl])
        ig = _sigmoid(pre[:, GATE_GROUP:2 * GATE_GROUP] + bx_ref[:, sl])
        log_a = r * neg_c_softplus[:, sl]
        a = jnp.exp(log_a)
        a_ref[:, sl] = a
        b_ref[:, sl] = _sqrt_nonneg(_one_minus_exp(2.0 * log_a, a * a)) * (ig * xc[:, sl])

    h = h_ref[...]
    for t in range(tc):
        sl = slice(t * nb, (t + 1) * nb)
        h = a_ref[sl, :] * h + b_ref[sl, :]
        b_ref[sl, :] = h
    h_ref[...] = h

    y_ref[...] = (b_ref[...] * gate).astype(y_ref.dtype)


def rglru_mixer(x, g, w_in, conv_w, conv_b, w_gates, b_a, b_x, lam, *, batch, tc=32):
    rows, d = x.shape
    assert batch % SUBLANES == 0, "time-major slabs must be whole sublane tiles"
    seq = rows // batch
    w = w_in.shape[1] // 2
    tc = min(tc, seq)
    assert tc >= RG_CONV - 1
    row = lambda v: v.reshape(1, -1)
    return pl.pallas_call(
        functools.partial(_rglru_kernel, tc=tc, nb=batch),
        grid=(seq // tc,),
        in_specs=[pl.BlockSpec((tc * batch, d), lambda i: (i, 0)),
                  _resident((1, d)),
                  _resident((d, 2 * w)),
                  _resident((RG_CONV, w)),
                  _resident((1, w)),
                  _resident(w_gates.shape),
                  _resident((1, w)),
                  _resident((1, w)),
                  _resident((1, w))],
        out_specs=pl.BlockSpec((tc * batch, w), lambda i: (i, 0)),
        out_shape=jax.ShapeDtypeStruct((rows, w), _BF16),
        scratch_shapes=[pltpu.VMEM(((tc + RG_CONV - 1) * batch, w), _F32),
                        pltpu.VMEM((tc * batch, w), _F32),
                        pltpu.VMEM((tc * batch, w), _F32),
                        pltpu.VMEM((batch, w), _F32)],
        compiler_params=pltpu.CompilerParams(dimension_semantics=("arbitrary",),
                                             vmem_limit_bytes=VMEM_LIMIT),
        name="rglru_mixer",
    )(x, row(g), w_in, conv_w, row(conv_b), w_gates, row(b_a), row(b_x), row(lam))


def _proj_mlp_kernel(x_ref, y_ref, wo_ref, g_ref, w1_ref, w2_ref, gf_ref, o_ref, x1_ref, h_ref,
                     *, ff_chunk, final_norm):
    x1_ref[...] = x_ref[...] + jnp.dot(y_ref[...], wo_ref[...], preferred_element_type=_F32)
    h_ref[...] = _rms_norm(x1_ref[...], g_ref[...]).astype(_BF16)
    d_ff = w1_ref.shape[1]
    for c in range(0, d_ff, ff_chunk):
        u = jnp.maximum(jnp.dot(h_ref[...], w1_ref[:, c:c + ff_chunk],
                                preferred_element_type=_F32), 0.0)
        x1_ref[...] += jnp.dot((u * u).astype(_BF16), w2_ref[c:c + ff_chunk, :],
                               preferred_element_type=_F32)
    out = x1_ref[...]
    if final_norm:
        out = _rms_norm(out, gf_ref[...])
    o_ref[...] = out


def proj_mlp(x, y, w_o, g, w1, w2, g_final, *, batch, seq, in_layout, out_layout, final_norm,
             tm=512, ff_chunk=1024):
    d, d_ff = w1.shape
    dy = w_o.shape[0]
    tm = min(tm, seq)
    ff_chunk = min(ff_chunk, d_ff)
    return pl.pallas_call(
        functools.partial(_proj_mlp_kernel, ff_chunk=ff_chunk, final_norm=final_norm),
        grid=(batch, seq // tm),
        in_specs=[_token_spec(in_layout, tm, d, seq),
                  _token_spec("sb", tm, dy, seq),
                  _resident(w_o.shape),
                  _resident((1, d)),
                  _resident((d, d_ff)),
                  _resident((d_ff, d)),
                  _resident((1, d))],
        out_specs=_token_spec(out_layout, tm, d, seq),
        out_shape=jax.ShapeDtypeStruct(_token_shape(out_layout, batch, seq, d), _F32),
        scratch_shapes=[pltpu.VMEM((tm, d), _F32),
                        pltpu.VMEM((tm, d), _BF16)],
        compiler_params=pltpu.CompilerParams(dimension_semantics=("parallel", "parallel"),
                                             vmem_limit_bytes=VMEM_LIMIT),
        name="proj_mlp",
    )(x, y, w_o, g.reshape(1, d), w1, w2, g_final.reshape(1, d))


def _block_diag_groups(w_blocks):
    nb, bd, _ = w_blocks.shape
    per = GATE_GROUP // bd
    wg = w_blocks.reshape(nb // per, per, bd, bd)
    eye = jnp.eye(per, dtype=w_blocks.dtype)
    return jnp.einsum("girc,ij->girjc", wg, eye).reshape(nb // per, GATE_GROUP, GATE_GROUP)


def kernel(x, norm_mix, norm_mlp, mlp_w1, mlp_w2, sb_w_qkv, sb_w_o, rg_w_in, rg_conv_w, rg_conv_b,
           rg_w_a, rg_b_a, rg_w_x, rg_b_x, rg_lambda, rg_w_o, norm_final):
    b, s, d = x.shape
    depth = norm_mix.shape[0]
    head_dim = d // SB_HEADS
    q_scale = jnp.concatenate([jnp.full((d,), math.log2(math.e) * head_dim ** -0.5, _F32),
                               jnp.ones((2 * d,), _F32)])

    cur, layout = x.reshape(b * s, d), "bs"
    ia = ib = 0
    for layer in range(depth):
        if layer % 2 == 0:
            w_qkv = (sb_w_qkv[ia] * q_scale).astype(_BF16)
            qkv = norm_proj(cur, norm_mix[layer], w_qkv, batch=b, seq=s, in_layout=layout)
            y = sb_attention(qkv, batch=b, head_dim=head_dim)
            w_o = sb_w_o[ia]
            ia += 1
        else:
            assert layout == "sb"
            w_gates = jnp.concatenate([_block_diag_groups(rg_w_a[ib]),
                                       _block_diag_groups(rg_w_x[ib])], axis=-1).astype(_BF16)
            y = rglru_mixer(cur.reshape(s * b, d), norm_mix[layer], rg_w_in[ib].astype(_BF16),
                            rg_conv_w[ib], rg_conv_b[ib], w_gates, rg_b_a[ib], rg_b_x[ib],
                            rg_lambda[ib], batch=b)
            y = y.reshape(s, b * y.shape[1])
            w_o = rg_w_o[ib]
            ib += 1
        out_layout = "bs" if layer == depth - 1 else "sb"
        cur = proj_mlp(cur, y, w_o.astype(_BF16), norm_mlp[layer], mlp_w1[layer].astype(_BF16),
                       mlp_w2[layer].astype(_BF16), norm_final, batch=b, seq=s, in_layout=layout,
                       out_layout=out_layout, final_norm=(layer == depth - 1))
        layout = out_layout
    return cur.reshape(b, s, d)
```

```python
import functools
import math

import jax
import jax.numpy as jnp
from jax import lax
from jax.experimental import pallas as pl
from jax.experimental.pallas import tpu as pltpu

EPS = 1e-6
SB_HEADS = 16
RG_CONV = 4
RG_C = 8.0
GATE_GROUP = 256
HEAD_BLOCK = 256
SUBLANES = 8
LANES = 128
VMEM_LIMIT = 56 * 1024 * 1024

_BF16 = jnp.bfloat16
_F32 = jnp.float32


def _rms_norm(x, g):
    ms = jnp.mean(x * x, axis=-1, keepdims=True)
    return x * lax.rsqrt(ms + EPS) * g


def _softplus(z):
    return jnp.maximum(z, 0.0) + jnp.log(1.0 + jnp.exp(-jnp.abs(z)))


def _sigmoid(z):
    return 0.5 * jnp.tanh(0.5 * z) + 0.5


def _sqrt_nonneg(x):
    return jnp.where(x > 0.0, x * lax.rsqrt(x), 0.0)


def _resident(shape):
    n = len(shape)
    return pl.BlockSpec(shape, lambda *_: (0,) * n, pipeline_mode=pl.Buffered(1))


def _token_shape(layout, batch, seq, d):
    return (batch * seq, d) if layout == "bs" else (seq, batch * d)


def _token_spec(layout, tm, d, seq):
    if layout == "bs":
        return pl.BlockSpec((tm, d), lambda b, j: (b * (seq // tm) + j, 0))
    return pl.BlockSpec((tm, d), lambda b, j: (j, b))


def _norm_proj_kernel(x_ref, g_ref, w_ref, o_ref, h_ref, *, n_chunk):
    h_ref[...] = _rms_norm(x_ref[...], g_ref[...]).astype(_BF16)
    n = w_ref.shape[1]
    for c in range(0, n, n_chunk):
        o_ref[:, c:c + n_chunk] = jnp.dot(
            h_ref[...], w_ref[:, c:c + n_chunk], preferred_element_type=_F32).astype(o_ref.dtype)


def norm_proj(x, g, w, *, batch, seq, in_layout, tm=512, n_chunk=512):
    d, n = w.shape
    tm = min(tm, seq)
    n_chunk = min(n_chunk, n)
    return pl.pallas_call(
        functools.partial(_norm_proj_kernel, n_chunk=n_chunk),
        grid=(batch, seq // tm),
        in_specs=[_token_spec(in_layout, tm, d, seq),
                  _resident((1, d)),
                  _resident((d, n))],
        out_specs=_token_spec("sb", tm, n, seq),
        out_shape=jax.ShapeDtypeStruct(_token_shape("sb", batch, seq, n), _BF16),
        scratch_shapes=[pltpu.VMEM((tm, d), _BF16)],
        compiler_params=pltpu.CompilerParams(dimension_semantics=("parallel", "parallel"),
                                             vmem_limit_bytes=VMEM_LIMIT),
        name="norm_proj",
    )(x, g.reshape(1, d), w)


def _softplus2(z):
    return jnp.maximum(z, 0.0) + jnp.log2(1.0 + jnp.exp2(-jnp.abs(z)))


_EXP2_ZERO_BELOW = -152.0


def _sb_scores(chains, tri):
    zs = [lax.dot_general(q_h, k_t, (((1,), (1,)), ((), ())), preferred_element_type=_F32)
          for q_h, k_t, _ in chains]
    hilos = []
    for z, (_, _, mask) in zip(zs, chains):
        sp = _softplus2(z)
        if mask is not None:
            sp = jnp.where(mask, sp, 0.0)
        hi = sp.astype(_BF16)
        lo = (sp - hi.astype(_F32)).astype(_BF16)
        hilos.append(jnp.concatenate([hi, lo], axis=1))
    incls = [jnp.dot(hilo, tri, preferred_element_type=_F32) for hilo in hilos]
    return zs, incls


def _sb_values(zs, incls, v_heads, runs, mask):
    ps, new_runs = [], []
    for h, (z, incl) in enumerate(zip(zs, incls)):
        e = z + incl
        tot = incl[:, 0:1]
        if runs is not None:
            e = e + runs[h]
            tot = tot + runs[h]
        p = jnp.exp2(e)
        if mask is not None:
            p = jnp.where(mask, p, 0.0)
        ps.append(p.astype(_BF16))
        new_runs.append(tot)
    pv = jnp.dot(jnp.concatenate(ps, axis=1), v_heads, preferred_element_type=_F32)
    return pv, new_runs


def _sb_attn_kernel(q_ref, k_ref, v_ref, tri_ref, o_ref, acc_ref, run_ref, *, tq, head_dim):
    i = pl.program_id(2)
    q = q_ref[...]
    n_heads = q.shape[1] // head_dim
    head_of_lane = lax.broadcasted_iota(jnp.int32, q.shape, 1) // head_dim
    zero = jnp.zeros_like(q)
    q_heads = [jnp.where(head_of_lane == h, q, zero) for h in range(n_heads)]
    tri = tri_ref[...]
    row = lax.broadcasted_iota(jnp.int32, (tq, tq), 0)
    col = lax.broadcasted_iota(jnp.int32, (tq, tq), 1)
    causal = col < row

    def k_tile(j):
        return k_ref[pl.ds(pl.multiple_of(j * tq, tq), tq), :]

    def v_tile(j):
        v_t = v_ref[pl.ds(pl.multiple_of(j * tq, tq), tq), :]
        return jnp.concatenate([jnp.where(head_of_lane == h, v_t, zero)
                                for h in range(n_heads)], axis=0)

    @pl.when(i == 0)
    def _():
        k_t = k_tile(0)
        zs, incls = _sb_scores([(q_h, k_t, causal) for q_h in q_heads], tri)
        pv, _ = _sb_values(zs, incls, v_tile(0), None, causal)
        o_ref[...] = pv.astype(o_ref.dtype)

    @pl.when(i > 0)
    def _():
        k_d, k_p = k_tile(i), k_tile(i - 1)
        zs, incls = _sb_scores([(q_h, k_d, causal) for q_h in q_heads]
                               + [(q_h, k_p, None) for q_h in q_heads], tri)
        pv_d, runs = _sb_values(zs[:n_heads], incls[:n_heads], v_tile(i), None, causal)
        pv_p, runs = _sb_values(zs[n_heads:], incls[n_heads:], v_tile(i - 1), runs, None)
        acc_ref[...] = pv_d + pv_p

        def store_runs(runs):
            top = runs[0]
            for h in range(n_heads):
                run_ref[h] = runs[h]
                top = jnp.maximum(top, runs[h])
            return jnp.max(top)

        def cond(carry):
            j, top_run = carry
            return jnp.logical_and(j >= 0, top_run >= _EXP2_ZERO_BELOW)

        def body(carry):
            j, _ = carry
            k_t = k_tile(j)
            zs, incls = _sb_scores([(q_h, k_t, None) for q_h in q_heads], tri)
            pv, runs = _sb_values(zs, incls, v_tile(j), [run_ref[h] for h in range(n_heads)],
                                  None)
            acc_ref[...] += pv
            return j - 1, store_runs(runs)

        lax.while_loop(cond, body, (i - 2, store_runs(runs)))
        o_ref[...] = acc_ref[...].astype(o_ref.dtype)


def sb_attention(qkv, *, batch, head_dim, tq=256, width=HEAD_BLOCK):
    s = qkv.shape[0]
    d = qkv.shape[1] // (3 * batch)
    n_blk = d // width
    tq = min(tq, s)
    r = lax.broadcasted_iota(jnp.int32, (2 * tq, tq), 0) % tq
    c = lax.broadcasted_iota(jnp.int32, (2 * tq, tq), 1)
    tri = jnp.where(r >= c, -1.0, 0.0).astype(_BF16)
    return pl.pallas_call(
        functools.partial(_sb_attn_kernel, tq=tq, head_dim=head_dim),
        grid=(batch, n_blk, s // tq),
        in_specs=[pl.BlockSpec((tq, width), lambda b, p, i: (i, b * 3 * n_blk + p)),
                  pl.BlockSpec((s, width), lambda b, p, i: (0, b * 3 * n_blk + n_blk + p)),
                  pl.BlockSpec((s, width), lambda b, p, i: (0, b * 3 * n_blk + 2 * n_blk + p)),
                  _resident((2 * tq, tq))],
        out_specs=pl.BlockSpec((tq, width), lambda b, p, i: (i, b * n_blk + p)),
        out_shape=jax.ShapeDtypeStruct((s, batch * d), _BF16),
        scratch_shapes=[pltpu.VMEM((tq, width), _F32),
                        pltpu.VMEM((width // head_dim, tq, 1), _F32)],
        compiler_params=pltpu.CompilerParams(
            dimension_semantics=("parallel", "parallel", "arbitrary"),
            vmem_limit_bytes=VMEM_LIMIT),
        name="sb_attention",
    )(qkv, qkv, qkv, tri)


def _gelu_tanh(x):
    c = math.sqrt(2.0 / math.pi)
    return 0.5 * x * (1.0 + jnp.tanh(c * (x + 0.044715 * (x * x * x))))


def _one_minus_exp(y, u):
    d = u - 1.0
    return jnp.where(d == 0.0, -y, -d * y / jnp.log1p(d))


def _rglru_kernel(x_ref, g_ref, win_ref, cw_ref, cb_ref, wg_ref, ba_ref, bx_ref, lam_ref, y_ref,
                  xt_ref, yt_ref, xs_ref, a_ref, b_ref, h_ref, *, tc, nb):
    d = x_ref.shape[1] // nb
    w = y_ref.shape[1] // nb
    rows = tc * nb
    tail = (RG_CONV - 1) * nb

    @pl.when(pl.program_id(0) == 0)
    def _():
        xs_ref[0:tail, :] = jnp.zeros((tail, w), _F32)
        h_ref[...] = jnp.zeros_like(h_ref)

    for bi in range(nb):
        for c in range(d // LANES):
            xt_ref[c, pl.ds(bi, tc, stride=nb), :] = x_ref[:, bi * d + c * LANES:
                                                           bi * d + (c + 1) * LANES]
    x_t = jnp.concatenate([xt_ref[c] for c in range(d // LANES)], axis=1)
    hn = _rms_norm(x_t, g_ref[...]).astype(_BF16)
    gate = _gelu_tanh(jnp.dot(hn, win_ref[:, 0:w], preferred_element_type=_F32))
    xs_ref[tail:tail + rows, :] = jnp.dot(hn, win_ref[:, w:2 * w], preferred_element_type=_F32)

    xc = cb_ref[...]
    for tap in range(RG_CONV):
        xc = xc + xs_ref[tap * nb:tap * nb + rows, :] * cw_ref[tap:tap + 1, :]
    xs_ref[0:tail, :] = xs_ref[rows:rows + tail, :]

    neg_c_softplus = -RG_C * _softplus(-lam_ref[...])
    xcb = xc.astype(_BF16)
    for gi in range(w // GATE_GROUP):
        sl = slice(gi * GATE_GROUP, (gi + 1) * GATE_GROUP)
        pre = jnp.dot(xcb[:, sl], wg_ref[gi], preferred_element_type=_F32)
        r = _sigmoid(pre[:, 0:GATE_GROUP] + ba_ref[:, sl])
        ig = _sigmoid(pre[:, GATE_GROUP:2 * GATE_GROUP] + bx_ref[:, sl])
        log_a = r * neg_c_softplus[:, sl]
        a = jnp.exp(log_a)
        a_ref[:, sl] = a
        b_ref[:, sl] = _sqrt_nonneg(_one_minus_exp(2.0 * log_a, a * a)) * (ig * xc[:, sl])

    h = h_ref[...]
    for t in range(tc):
        sl = slice(t * nb, (t + 1) * nb)
        h = a_ref[sl, :] * h + b_ref[sl, :]
        b_ref[sl, :] = h
    h_ref[...] = h

    y_t = b_ref[...] * gate
    for c in range(w // LANES):
        yt_ref[c] = y_t[:, c * LANES:(c + 1) * LANES]
    for bi in range(nb):
        for c in range(w // LANES):
            y_ref[:, bi * w + c * LANES:bi * w + (c + 1) * LANES] = (
                yt_ref[c, pl.ds(bi, tc, stride=nb), :].astype(y_ref.dtype))


def rglru_mixer(x, g, w_in, conv_w, conv_b, w_gates, b_a, b_x, lam, *, batch, tc=32):
    seq = x.shape[0]
    d = x.shape[1] // batch
    assert batch % SUBLANES == 0, "time-step slabs must be whole sublane tiles"
    w = w_in.shape[1] // 2
    tc = min(tc, seq)
    assert tc >= RG_CONV - 1
    row = lambda v: v.reshape(1, -1)
    return pl.pallas_call(
        functools.partial(_rglru_kernel, tc=tc, nb=batch),
        grid=(seq // tc,),
        in_specs=[pl.BlockSpec((tc, batch * d), lambda i: (i, 0)),
                  _resident((1, d)),
                  _resident((d, 2 * w)),
                  _resident((RG_CONV, w)),
                  _resident((1, w)),
                  _resident(w_gates.shape),
                  _resident((1, w)),
                  _resident((1, w)),
                  _resident((1, w))],
        out_specs=pl.BlockSpec((tc, batch * w), lambda i: (i, 0)),
        out_shape=jax.ShapeDtypeStruct((seq, batch * w), _BF16),
        scratch_shapes=[pltpu.VMEM((d // LANES, tc * batch, LANES), _F32),
                        pltpu.VMEM((w // LANES, tc * batch, LANES), _F32),
                        pltpu.VMEM(((tc + RG_CONV - 1) * batch, w), _F32),
                        pltpu.VMEM((tc * batch, w), _F32),
                        pltpu.VMEM((tc * batch, w), _F32),
                        pltpu.VMEM((batch, w), _F32)],
        compiler_params=pltpu.CompilerParams(dimension_semantics=("arbitrary",),
                                             vmem_limit_bytes=VMEM_LIMIT),
        name="rglru_mixer",
    )(x, row(g), w_in, conv_w, row(conv_b), w_gates, row(b_a), row(b_x), row(lam))


def _proj_mlp_kernel(x_ref, y_ref, wo_ref, g_ref, w1_ref, w2_ref, gf_ref, o_ref, x1_ref, h_ref,
                     *, ff_chunk, final_norm):
    x1_ref[...] = x_ref[...] + jnp.dot(y_ref[...], wo_ref[...], preferred_element_type=_F32)
    h_ref[...] = _rms_norm(x1_ref[...], g_ref[...]).astype(_BF16)
    d_ff = w1_ref.shape[1]
    for c in range(0, d_ff, ff_chunk):
        u = jnp.maximum(jnp.dot(h_ref[...], w1_ref[:, c:c + ff_chunk],
                                preferred_element_type=_F32), 0.0)
        x1_ref[...] += jnp.dot((u * u).astype(_BF16), w2_ref[c:c + ff_chunk, :],
                               preferred_element_type=_F32)
    out = x1_ref[...]
    if final_norm:
        out = _rms_norm(out, gf_ref[...])
    o_ref[...] = out


def proj_mlp(x, y, w_o, g, w1, w2, g_final, *, batch, seq, in_layout, out_layout, final_norm,
             tm=512, ff_chunk=1024):
    d, d_ff = w1.shape
    dy = w_o.shape[0]
    tm = min(tm, seq)
    ff_chunk = min(ff_chunk, d_ff)
    return pl.pallas_call(
        functools.partial(_proj_mlp_kernel, ff_chunk=ff_chunk, final_norm=final_norm),
        grid=(batch, seq // tm),
        in_specs=[_token_spec(in_layout, tm, d, seq),
                  _token_spec("sb", tm, dy, seq),
                  _resident(w_o.shape),
                  _resident((1, d)),
                  _resident((d, d_ff)),
                  _resident((d_ff, d)),
                  _resident((1, d))],
        out_specs=_token_spec(out_layout, tm, d, seq),
        out_shape=jax.ShapeDtypeStruct(_token_shape(out_layout, batch, seq, d), _F32),
        scratch_shapes=[pltpu.VMEM((tm, d), _F32),
                        pltpu.VMEM((tm, d), _BF16)],
        compiler_params=pltpu.CompilerParams(dimension_semantics=("parallel", "parallel"),
                                             vmem_limit_bytes=VMEM_LIMIT),
        name="proj_mlp",
    )(x, y, w_o, g.reshape(1, d), w1, w2, g_final.reshape(1, d))


def _block_diag_groups(w_blocks):
    nb, bd, _ = w_blocks.shape
    per = GATE_GROUP // bd
    wg = w_blocks.reshape(nb // per, per, bd, bd)
    eye = jnp.eye(per, dtype=w_blocks.dtype)
    return jnp.einsum("girc,ij->girjc", wg, eye).reshape(nb // per, GATE_GROUP, GATE_GROUP)


def kernel(x, norm_mix, norm_mlp, mlp_w1, mlp_w2, sb_w_qkv, sb_w_o, rg_w_in, rg_conv_w, rg_conv_b,
           rg_w_a, rg_b_a, rg_w_x, rg_b_x, rg_lambda, rg_w_o, norm_final):
    b, s, d = x.shape
    depth = norm_mix.shape[0]
    head_dim = d // SB_HEADS
    q_scale = jnp.concatenate([jnp.full((d,), math.log2(math.e) * head_dim ** -0.5, _F32),
                               jnp.ones((2 * d,), _F32)])

    cur, layout = x.reshape(b * s, d), "bs"
    ia = ib = 0
    for layer in range(depth):
        if layer % 2 == 0:
            w_qkv = (sb_w_qkv[ia] * q_scale).astype(_BF16)
            qkv = norm_proj(cur, norm_mix[layer], w_qkv, batch=b, seq=s, in_layout=layout)
            y = sb_attention(qkv, batch=b, head_dim=head_dim)
            w_o = sb_w_o[ia]
            ia += 1
        else:
            assert layout == "sb"
            w_gates = jnp.concatenate([_block_diag_groups(rg_w_a[ib]),
                                       _block_diag_groups(rg_w_x[ib])], axis=-1).astype(_BF16)
            y = rglru_mixer(cur, norm_mix[layer], rg_w_in[ib].astype(_BF16), rg_conv_w[ib],
                            rg_conv_b[ib], w_gates, rg_b_a[ib], rg_b_x[ib], rg_lambda[ib],
                            batch=b)
            w_o = rg_w_o[ib]
            ib += 1
        out_layout = "bs" if layer == depth - 1 else "sb"
        cur = proj_mlp(cur, y, w_o.astype(_BF16), norm_mlp[layer], mlp_w1[layer].astype(_BF16),
                       mlp_w2[layer].astype(_BF16), norm_final, batch=b, seq=s, in_layout=layout,
                       out_layout=out_layout, final_norm=(layer == depth - 1))
        layout = out_layout
    return cur.reshape(b, s, d)
```

```python
import functools
import math

import jax
import jax.numpy as jnp
from jax import lax
from jax.experimental import pallas as pl
from jax.experimental.pallas import tpu as pltpu

EPS = 1e-6
SB_HEADS = 16
RG_CONV = 4
RG_C = 8.0
GATE_GROUP = 256
HEAD_BLOCK = 256
Q_TILES = 4
SUBLANES = 8
LANES = 128
VMEM_LIMIT = 56 * 1024 * 1024

_BF16 = jnp.bfloat16
_F32 = jnp.float32


def _rms_norm(x, g):
    ms = jnp.mean(x * x, axis=-1, keepdims=True)
    return x * lax.rsqrt(ms + EPS) * g


def _softplus(z):
    return jnp.maximum(z, 0.0) + jnp.log(1.0 + jnp.exp(-jnp.abs(z)))


def _sigmoid(z):
    return 0.5 * jnp.tanh(0.5 * z) + 0.5


def _sqrt_nonneg(x):
    return jnp.where(x > 0.0, x * lax.rsqrt(x), 0.0)


def _resident(shape):
    n = len(shape)
    return pl.BlockSpec(shape, lambda *_: (0,) * n, pipeline_mode=pl.Buffered(1))


def _token_shape(layout, batch, seq, d):
    return (batch * seq, d) if layout == "bs" else (seq, batch * d)


def _token_spec(layout, tm, d, seq):
    if layout == "bs":
        return pl.BlockSpec((tm, d), lambda b, j: (b * (seq // tm) + j, 0))
    return pl.BlockSpec((tm, d), lambda b, j: (j, b))


def _norm_proj_kernel(x_ref, g_ref, w_ref, o_ref, h_ref, *, n_chunk):
    h_ref[...] = _rms_norm(x_ref[...], g_ref[...]).astype(_BF16)
    n = w_ref.shape[1]
    for c in range(0, n, n_chunk):
        o_ref[:, c:c + n_chunk] = jnp.dot(
            h_ref[...], w_ref[:, c:c + n_chunk], preferred_element_type=_F32).astype(o_ref.dtype)


def norm_proj(x, g, w, *, batch, seq, in_layout, tm=512, n_chunk=512):
    d, n = w.shape
    tm = min(tm, seq)
    n_chunk = min(n_chunk, n)
    return pl.pallas_call(
        functools.partial(_norm_proj_kernel, n_chunk=n_chunk),
        grid=(batch, seq // tm),
        in_specs=[_token_spec(in_layout, tm, d, seq),
                  _resident((1, d)),
                  _resident((d, n))],
        out_specs=_token_spec("sb", tm, n, seq),
        out_shape=jax.ShapeDtypeStruct(_token_shape("sb", batch, seq, n), _BF16),
        scratch_shapes=[pltpu.VMEM((tm, d), _BF16)],
        compiler_params=pltpu.CompilerParams(dimension_semantics=("parallel", "parallel"),
                                             vmem_limit_bytes=VMEM_LIMIT),
        name="norm_proj",
    )(x, g.reshape(1, d), w)


def _softplus2(z):
    return jnp.maximum(z, 0.0) + jnp.log2(1.0 + jnp.exp2(-jnp.abs(z)))


_EXP2_ZERO_BELOW = -152.0


def _sb_qk(q_heads, k_t):
    return [lax.dot_general(q_h, k_t, (((1,), (1,)), ((), ())), preferred_element_type=_F32)
            for q_h in q_heads]


def _sb_keep(zs, mask):
    hilos = []
    for z in zs:
        sp = _softplus2(z)
        if mask is not None:
            sp = jnp.where(mask, sp, 0.0)
        hi = sp.astype(_BF16)
        lo = (sp - hi.astype(_F32)).astype(_BF16)
        hilos.append(jnp.concatenate([hi, lo], axis=1))
    return hilos


def _sb_cumsum(hilos, tri):
    return [jnp.dot(hilo, tri, preferred_element_type=_F32) for hilo in hilos]


def _sb_values(zs, incls, v_heads, runs, mask, pv_dots=1):
    ps, new_runs = [], []
    for h, (z, incl) in enumerate(zip(zs, incls)):
        e = z + incl
        tot = incl[:, 0:1]
        if runs is not None:
            e = e + runs[h]
            tot = tot + runs[h]
        p = jnp.exp2(e)
        if mask is not None:
            p = jnp.where(mask, p, 0.0)
        ps.append(p.astype(_BF16))
        new_runs.append(tot)
    per = len(ps) // pv_dots
    rows = per * v_heads.shape[0] // len(ps)
    pv = None
    for c in range(pv_dots):
        part = jnp.dot(jnp.concatenate(ps[c * per:(c + 1) * per], axis=1),
                       v_heads[c * rows:(c + 1) * rows], preferred_element_type=_F32)
        pv = part if pv is None else pv + part
    return pv, new_runs


def _sb_attn_kernel(q_ref, k_ref, v_ref, tri_ref, o_ref, acc_ref, run_ref, *, tq, head_dim):
    step = pl.program_id(2)
    width = q_ref.shape[1]
    n_heads = width // head_dim
    head_of_lane = lax.broadcasted_iota(jnp.int32, (tq, width), 1) // head_dim
    zero = jnp.zeros((tq, width), q_ref.dtype)
    tri = tri_ref[...]
    row = lax.broadcasted_iota(jnp.int32, (tq, tq), 0)
    col = lax.broadcasted_iota(jnp.int32, (tq, tq), 1)
    causal = col < row

    def q_heads(t):
        q = q_ref[t * tq:(t + 1) * tq, :]
        return [jnp.where(head_of_lane == h, q, zero) for h in range(n_heads)]

    def k_tile(j):
        return k_ref[pl.ds(pl.multiple_of(j * tq, tq), tq), :]

    def v_tile(j):
        v_t = v_ref[pl.ds(pl.multiple_of(j * tq, tq), tq), :]
        return jnp.concatenate([jnp.where(head_of_lane == h, v_t, zero)
                                for h in range(n_heads)], axis=0)

    def diagonal_tile(t, j):
        qh = q_heads(t)
        zs = _sb_qk(qh, k_tile(j))
        incls = _sb_cumsum(_sb_keep(zs, causal), tri)
        return _sb_values(zs, incls, v_tile(j), None, causal, n_heads)

    def diagonal_and_previous_tile(t, j):
        qh = q_heads(t)
        zs_d, zs_p = _sb_qk(qh, k_tile(j)), _sb_qk(qh, k_tile(j - 1))
        incls_d = _sb_cumsum(_sb_keep(zs_d, causal), tri)
        incls_p = _sb_cumsum(_sb_keep(zs_p, None), tri)
        pv_d, runs = _sb_values(zs_d, incls_d, v_tile(j), None, causal, n_heads)
        pv_p, runs = _sb_values(zs_p, incls_p, v_tile(j - 1), runs, None, n_heads)
        return pv_d + pv_p, runs

    def store_runs(t, runs):
        top = runs[0]
        for h in range(n_heads):
            run_ref[t, h] = runs[h]
            top = jnp.maximum(top, runs[h])
        return jnp.max(top)

    def earlier_tiles(t, j_first, top_run):
        def cond(carry):
            j, top_run = carry
            return jnp.logical_and(j >= 0, top_run >= _EXP2_ZERO_BELOW)

        def body(carry):
            j, _ = carry
            zs = _sb_qk(q_heads(t), k_tile(j))
            incls = _sb_cumsum(_sb_keep(zs, None), tri)
            pv, runs = _sb_values(zs, incls, v_tile(j),
                                  [run_ref[t, h] for h in range(n_heads)], None)
            acc_ref[t] += pv
            return j - 1, store_runs(t, runs)

        lax.while_loop(cond, body, (j_first, top_run))

    @pl.when(step == 0)
    def _():
        pv, _ = diagonal_tile(0, 0)
        acc_ref[0] = pv
        tops = [None]
        for t in range(1, Q_TILES):
            pv, runs = diagonal_and_previous_tile(t, t)
            acc_ref[t] = pv
            tops.append(store_runs(t, runs) if t >= 2 else None)
        for t in range(Q_TILES):
            if t >= 2:
                earlier_tiles(t, t - 2, tops[t])
            o_ref[t * tq:(t + 1) * tq, :] = acc_ref[t].astype(o_ref.dtype)

    @pl.when(step > 0)
    def _():
        first = step * Q_TILES
        tops = []
        for t in range(Q_TILES):
            pv, runs = diagonal_and_previous_tile(t, first + t)
            acc_ref[t] = pv
            tops.append(store_runs(t, runs))
        for t in range(Q_TILES):
            earlier_tiles(t, first + t - 2, tops[t])
            o_ref[t * tq:(t + 1) * tq, :] = acc_ref[t].astype(o_ref.dtype)


def sb_attention(qkv, *, batch, head_dim, tq=256, width=HEAD_BLOCK):
    s = qkv.shape[0]
    d = qkv.shape[1] // (3 * batch)
    n_blk = d // width
    tq = min(tq, s // Q_TILES)
    rows = tq * Q_TILES
    r = lax.broadcasted_iota(jnp.int32, (2 * tq, tq), 0) % tq
    c = lax.broadcasted_iota(jnp.int32, (2 * tq, tq), 1)
    tri = jnp.where(r >= c, -1.0, 0.0).astype(_BF16)
    return pl.pallas_call(
        functools.partial(_sb_attn_kernel, tq=tq, head_dim=head_dim),
        grid=(batch, n_blk, s // rows),
        in_specs=[pl.BlockSpec((rows, width), lambda b, p, i: (i, b * 3 * n_blk + p)),
                  pl.BlockSpec((s, width), lambda b, p, i: (0, b * 3 * n_blk + n_blk + p)),
                  pl.BlockSpec((s, width), lambda b, p, i: (0, b * 3 * n_blk + 2 * n_blk + p)),
                  _resident((2 * tq, tq))],
        out_specs=pl.BlockSpec((rows, width), lambda b, p, i: (i, b * n_blk + p)),
        out_shape=jax.ShapeDtypeStruct((s, batch * d), _BF16),
        scratch_shapes=[pltpu.VMEM((Q_TILES, tq, width), _F32),
                        pltpu.VMEM((Q_TILES, width // head_dim, tq, 1), _F32)],
        compiler_params=pltpu.CompilerParams(
            dimension_semantics=("parallel", "parallel", "arbitrary"),
            vmem_limit_bytes=VMEM_LIMIT),
        name="sb_attention",
    )(qkv, qkv, qkv, tri)


def _gelu_tanh(x):
    c = math.sqrt(2.0 / math.pi)
    return 0.5 * x * (1.0 + jnp.tanh(c * (x + 0.044715 * (x * x * x))))


def _one_minus_exp(y, u):
    d = u - 1.0
    return jnp.where(d == 0.0, -y, -d * y / jnp.log1p(d))


def _rglru_kernel(x_ref, g_ref, win_ref, cw_ref, cb_ref, wg_ref, ba_ref, bx_ref, lam_ref, y_ref,
                  xt_ref, yt_ref, xs_ref, a_ref, b_ref, h_ref, *, tc, nb):
    d = x_ref.shape[1] // nb
    w = y_ref.shape[1] // nb
    rows = tc * nb
    tail = (RG_CONV - 1) * nb

    @pl.when(pl.program_id(0) == 0)
    def _():
        xs_ref[0:tail, :] = jnp.zeros((tail, w), _F32)
        h_ref[...] = jnp.zeros_like(h_ref)

    for bi in range(nb):
        for c in range(d // LANES):
            xt_ref[c, pl.ds(bi, tc, stride=nb), :] = x_ref[:, bi * d + c * LANES:
                                                           bi * d + (c + 1) * LANES]
    x_t = jnp.concatenate([xt_ref[c] for c in range(d // LANES)], axis=1)
    hn = _rms_norm(x_t, g_ref[...]).astype(_BF16)
    gate = _gelu_tanh(jnp.dot(hn, win_ref[:, 0:w], preferred_element_type=_F32))
    xs_ref[tail:tail + rows, :] = jnp.dot(hn, win_ref[:, w:2 * w], preferred_element_type=_F32)

    xc = cb_ref[...]
    for tap in range(RG_CONV):
        xc = xc + xs_ref[tap * nb:tap * nb + rows, :] * cw_ref[tap:tap + 1, :]
    xs_ref[0:tail, :] = xs_ref[rows:rows + tail, :]

    neg_c_softplus = -RG_C * _softplus(-lam_ref[...])
    xcb = xc.astype(_BF16)
    for gi in range(w // GATE_GROUP):
        sl = slice(gi * GATE_GROUP, (gi + 1) * GATE_GROUP)
        pre = jnp.dot(xcb[:, sl], wg_ref[gi], preferred_element_type=_F32)
        r = _sigmoid(pre[:, 0:GATE_GROUP] + ba_ref[:, sl])
        ig = _sigmoid(pre[:, GATE_GROUP:2 * GATE_GROUP] + bx_ref[:, sl])
        log_a = r * neg_c_softplus[:, sl]
        a = jnp.exp(log_a)
        a_ref[:, sl] = a
        b_ref[:, sl] = _sqrt_nonneg(_one_minus_exp(2.0 * log_a, a * a)) * (ig * xc[:, sl])

    h = h_ref[...]
    for t in range(tc):
        sl = slice(t * nb, (t + 1) * nb)
        h = a_ref[sl, :] * h + b_ref[sl, :]
        b_ref[sl, :] = h
    h_ref[...] = h

    y_t = b_ref[...] * gate
    for c in range(w // LANES):
        yt_ref[c] = y_t[:, c * LANES:(c + 1) * LANES]
    for bi in range(nb):
        for c in range(w // LANES):
            y_ref[:, bi * w + c * LANES:bi * w + (c + 1) * LANES] = (
                yt_ref[c, pl.ds(bi, tc, stride=nb), :].astype(y_ref.dtype))


def rglru_mixer(x, g, w_in, conv_w, conv_b, w_gates, b_a, b_x, lam, *, batch, tc=32):
    seq = x.shape[0]
    d = x.shape[1] // batch
    assert batch % SUBLANES == 0, "time-step slabs must be whole sublane tiles"
    w = w_in.shape[1] // 2
    tc = min(tc, seq)
    assert tc >= RG_CONV - 1
    row = lambda v: v.reshape(1, -1)
    return pl.pallas_call(
        functools.partial(_rglru_kernel, tc=tc, nb=batch),
        grid=(seq // tc,),
        in_specs=[pl.BlockSpec((tc, batch * d), lambda i: (i, 0)),
                  _resident((1, d)),
                  _resident((d, 2 * w)),
                  _resident((RG_CONV, w)),
                  _resident((1, w)),
                  _resident(w_gates.shape),
                  _resident((1, w)),
                  _resident((1, w)),
                  _resident((1, w))],
        out_specs=pl.BlockSpec((tc, batch * w), lambda i: (i, 0)),
        out_shape=jax.ShapeDtypeStruct((seq, batch * w), _BF16),
        scratch_shapes=[pltpu.VMEM((d // LANES, tc * batch, LANES), _F32),
                        pltpu.VMEM((w // LANES, tc * batch, LANES), _F32),
                        pltpu.VMEM(((tc + RG_CONV - 1) * batch, w), _F32),
                        pltpu.VMEM((tc * batch, w), _F32),
                        pltpu.VMEM((tc * batch, w), _F32),
                        pltpu.VMEM((batch, w), _F32)],
        compiler_params=pltpu.CompilerParams(dimension_semantics=("arbitrary",),
                                             vmem_limit_bytes=VMEM_LIMIT),
        name="rglru_mixer",
    )(x, row(g), w_in, conv_w, row(conv_b), w_gates, row(b_a), row(b_x), row(lam))


def _proj_mlp_kernel(x_ref, y_ref, wo_ref, g_ref, w1_ref, w2_ref, gf_ref, o_ref, x1_ref, h_ref,
                     *, ff_chunk, final_norm):
    x1_ref[...] = x_ref[...] + jnp.dot(y_ref[...], wo_ref[...], preferred_element_type=_F32)
    h_ref[...] = _rms_norm(x1_ref[...], g_ref[...]).astype(_BF16)
    d_ff = w1_ref.shape[1]
    for c in range(0, d_ff, ff_chunk):
        u = jnp.maximum(jnp.dot(h_ref[...], w1_ref[:, c:c + ff_chunk],
                                preferred_element_type=_F32), 0.0)
        x1_ref[...] += jnp.dot((u * u).astype(_BF16), w2_ref[c:c + ff_chunk, :],
                               preferred_element_type=_F32)
    out = x1_ref[...]
    if final_norm:
        out = _rms_norm(out, gf_ref[...])
    o_ref[...] = out


def proj_mlp(x, y, w_o, g, w1, w2, g_final, *, batch, seq, in_layout, out_layout, final_norm,
             tm=512, ff_chunk=1024):
    d, d_ff = w1.shape
    dy = w_o.shape[0]
    tm = min(tm, seq)
    ff_chunk = min(ff_chunk, d_ff)
    return pl.pallas_call(
        functools.partial(_proj_mlp_kernel, ff_chunk=ff_chunk, final_norm=final_norm),
        grid=(batch, seq // tm),
        in_specs=[_token_spec(in_layout, tm, d, seq),
                  _token_spec("sb", tm, dy, seq),
                  _resident(w_o.shape),
                  _resident((1, d)),
                  _resident((d, d_ff)),
                  _resident((d_ff, d)),
                  _resident((1, d))],
        out_specs=_token_spec(out_layout, tm, d, seq),
        out_shape=jax.ShapeDtypeStruct(_token_shape(out_layout, batch, seq, d), _F32),
        scratch_shapes=[pltpu.VMEM((tm, d), _F32),
                        pltpu.VMEM((tm, d), _BF16)],
        compiler_params=pltpu.CompilerParams(dimension_semantics=("parallel", "parallel"),
                                             vmem_limit_bytes=VMEM_LIMIT),
        name="proj_mlp",
    )(x, y, w_o, g.reshape(1, d), w1, w2, g_final.reshape(1, d))


def _block_diag_groups(w_blocks):
    nb, bd, _ = w_blocks.shape
    per = GATE_GROUP // bd
    wg = w_blocks.reshape(nb // per, per, bd, bd)
    eye = jnp.eye(per, dtype=w_blocks.dtype)
    return jnp.einsum("girc,ij->girjc", wg, eye).reshape(nb // per, GATE_GROUP, GATE_GROUP)


def kernel(x, norm_mix, norm_mlp, mlp_w1, mlp_w2, sb_w_qkv, sb_w_o, rg_w_in, rg_conv_w, rg_conv_b,
           rg_w_a, rg_b_a, rg_w_x, rg_b_x, rg_lambda, rg_w_o, norm_final):
    b, s, d = x.shape
    depth = norm_mix.shape[0]
    head_dim = d // SB_HEADS
    q_scale = jnp.concatenate([jnp.full((d,), math.log2(math.e) * head_dim ** -0.5, _F32),
                               jnp.ones((2 * d,), _F32)])

    cur, layout = x.reshape(b * s, d), "bs"
    ia = ib = 0
    for layer in range(depth):
        if layer % 2 == 0:
            w_qkv = (sb_w_qkv[ia] * q_scale).astype(_BF16)
            qkv = norm_proj(cur, norm_mix[layer], w_qkv, batch=b, seq=s, in_layout=layout)
            y = sb_attention(qkv, batch=b, head_dim=head_dim)
            w_o = sb_w_o[ia]
            ia += 1
        else:
            assert layout == "sb"
            w_gates = jnp.concatenate([_block_diag_groups(rg_w_a[ib]),
                                       _block_diag_groups(rg_w_x[ib])], axis=-1).astype(_BF16)
            y = rglru_mixer(cur, norm_mix[layer], rg_w_in[ib].astype(_BF16), rg_conv_w[ib],
                            rg_conv_b[ib], w_gates, rg_b_a[ib], rg_b_x[ib], rg_lambda[ib],
                            batch=b)
            w_o = rg_w_o[ib]
            ib += 1
        out_layout = "bs" if layer == depth - 1 else "sb"
        cur = proj_mlp(cur, y, w_o.astype(_BF16), norm_mlp[layer], mlp_w1[layer].astype(_BF16),
                       mlp_w2[layer].astype(_BF16), norm_final, batch=b, seq=s, in_layout=layout,
                       out_layout=out_layout, final_norm=(layer == depth - 1))
        layout = out_layout
    return cur.reshape(b, s, d)
```

```python
import functools
import math

import jax
import jax.numpy as jnp
from jax import lax
from jax.experimental import pallas as pl
from jax.experimental.pallas import tpu as pltpu

EPS = 1e-6
SB_HEADS = 16
RG_CONV = 4
RG_C = 8.0
GATE_GROUP = 256
HEAD_BLOCK = 256
Q_TILES = 4
SUBLANES = 8
LANES = 128
VMEM_LIMIT = 56 * 1024 * 1024

_BF16 = jnp.bfloat16
_F32 = jnp.float32


def _rms_norm(x, g):
    ms = jnp.mean(x * x, axis=-1, keepdims=True)
    return x * lax.rsqrt(ms + EPS) * g


def _softplus(z):
    return jnp.maximum(z, 0.0) + jnp.log(1.0 + jnp.exp(-jnp.abs(z)))


def _sqrt_nonneg(x):
    return jnp.where(x > 0.0, x * lax.rsqrt(x), 0.0)


def _resident(shape):
    n = len(shape)
    return pl.BlockSpec(shape, lambda *_: (0,) * n, pipeline_mode=pl.Buffered(1))


def _resident_layer(stack_shape, layer):
    return pl.BlockSpec((None,) + tuple(stack_shape[1:]), lambda *_: (layer, 0, 0),
                        pipeline_mode=pl.Buffered(1))


def _token_shape(layout, batch, seq, d):
    return (batch * seq, d) if layout == "bs" else (seq, batch * d)


def _token_spec(layout, tm, d, seq):
    if layout == "bs":
        return pl.BlockSpec((tm, d), lambda b, j: (b * (seq // tm) + j, 0))
    return pl.BlockSpec((tm, d), lambda b, j: (j, b))


def _norm_proj_kernel(x_ref, g_ref, w_ref, o_ref, h_ref, *, n_chunk):
    h_ref[...] = _rms_norm(x_ref[...], g_ref[...]).astype(_BF16)
    n = w_ref.shape[1]
    for c in range(0, n, n_chunk):
        o_ref[:, c:c + n_chunk] = jnp.dot(
            h_ref[...], w_ref[:, c:c + n_chunk], preferred_element_type=_F32).astype(o_ref.dtype)


def norm_proj(x, g, w_stack, layer, *, batch, seq, in_layout, tm=512, n_chunk=512):
    _, d, n = w_stack.shape
    tm = min(tm, seq)
    n_chunk = min(n_chunk, n)
    return pl.pallas_call(
        functools.partial(_norm_proj_kernel, n_chunk=n_chunk),
        grid=(batch, seq // tm),
        in_specs=[_token_spec(in_layout, tm, d, seq),
                  _resident((1, d)),
                  _resident_layer(w_stack.shape, layer)],
        out_specs=_token_spec("sb", tm, n, seq),
        out_shape=jax.ShapeDtypeStruct(_token_shape("sb", batch, seq, n), _BF16),
        scratch_shapes=[pltpu.VMEM((tm, d), _BF16)],
        compiler_params=pltpu.CompilerParams(dimension_semantics=("parallel", "parallel"),
                                             vmem_limit_bytes=VMEM_LIMIT),
        name="norm_proj",
    )(x, g.reshape(1, d), w_stack)


def _softplus2(z):
    return jnp.maximum(z, 0.0) + jnp.log2(1.0 + jnp.exp2(-jnp.abs(z)))


_EXP2_ZERO_BELOW = -152.0


def _sb_qk(q_heads, k_t):
    return [lax.dot_general(q_h, k_t, (((1,), (1,)), ((), ())), preferred_element_type=_F32)
            for q_h in q_heads]


def _sb_keep(zs, mask):
    hilos = []
    for z in zs:
        sp = _softplus2(z)
        if mask is not None:
            sp = jnp.where(mask, sp, 0.0)
        hi = sp.astype(_BF16)
        lo = (sp - hi.astype(_F32)).astype(_BF16)
        hilos.append(jnp.concatenate([hi, lo], axis=1))
    return hilos


def _sb_cumsum(hilos, tri):
    return [jnp.dot(hilo, tri, preferred_element_type=_F32) for hilo in hilos]


def _sb_values(zs, incls, v_heads, runs, mask, pv_dots=1):
    ps, new_runs = [], []
    for h, (z, incl) in enumerate(zip(zs, incls)):
        e = z + incl
        tot = incl[:, 0:1]
        if runs is not None:
            e = e + runs[h]
            tot = tot + runs[h]
        p = jnp.exp2(e)
        if mask is not None:
            p = jnp.where(mask, p, 0.0)
        ps.append(p.astype(_BF16))
        new_runs.append(tot)
    per = len(ps) // pv_dots
    rows = per * v_heads.shape[0] // len(ps)
    pv = None
    for c in range(pv_dots):
        part = jnp.dot(jnp.concatenate(ps[c * per:(c + 1) * per], axis=1),
                       v_heads[c * rows:(c + 1) * rows], preferred_element_type=_F32)
        pv = part if pv is None else pv + part
    return pv, new_runs


def _sb_attn_kernel(q_ref, k_ref, v_ref, tri_ref, o_ref, acc_ref, run_ref, *, tq, head_dim):
    step = pl.program_id(2)
    width = q_ref.shape[1]
    n_heads = width // head_dim
    head_of_lane = lax.broadcasted_iota(jnp.int32, (tq, width), 1) // head_dim
    zero = jnp.zeros((tq, width), q_ref.dtype)
    tri = tri_ref[...]
    row = lax.broadcasted_iota(jnp.int32, (tq, tq), 0)
    col = lax.broadcasted_iota(jnp.int32, (tq, tq), 1)
    causal = col < row

    def q_heads(t):
        q = q_ref[t * tq:(t + 1) * tq, :]
        return [jnp.where(head_of_lane == h, q, zero) for h in range(n_heads)]

    def k_tile(j):
        return k_ref[pl.ds(pl.multiple_of(j * tq, tq), tq), :]

    def v_tile(j):
        v_t = v_ref[pl.ds(pl.multiple_of(j * tq, tq), tq), :]
        return jnp.concatenate([jnp.where(head_of_lane == h, v_t, zero)
                                for h in range(n_heads)], axis=0)

    def diagonal_tile(t, j):
        qh = q_heads(t)
        zs = _sb_qk(qh, k_tile(j))
        incls = _sb_cumsum(_sb_keep(zs, causal), tri)
        return _sb_values(zs, incls, v_tile(j), None, causal, n_heads)

    def diagonal_and_previous_tile(t, j):
        qh = q_heads(t)
        zs_d, zs_p = _sb_qk(qh, k_tile(j)), _sb_qk(qh, k_tile(j - 1))
        incls_d = _sb_cumsum(_sb_keep(zs_d, causal), tri)
        incls_p = _sb_cumsum(_sb_keep(zs_p, None), tri)
        pv_d, runs = _sb_values(zs_d, incls_d, v_tile(j), None, causal, n_heads)
        pv_p, runs = _sb_values(zs_p, incls_p, v_tile(j - 1), runs, None, n_heads)
        return pv_d + pv_p, runs

    def store_runs(t, runs):
        top = runs[0]
        for h in range(n_heads):
            run_ref[t, h] = runs[h]
            top = jnp.maximum(top, runs[h])
        return jnp.max(top)

    def earlier_tiles(t, j_first, top_run):
        def cond(carry):
            j, top_run = carry
            return jnp.logical_and(j >= 0, top_run >= _EXP2_ZERO_BELOW)

        def body(carry):
            j, _ = carry
            zs = _sb_qk(q_heads(t), k_tile(j))
            incls = _sb_cumsum(_sb_keep(zs, None), tri)
            pv, runs = _sb_values(zs, incls, v_tile(j),
                                  [run_ref[t, h] for h in range(n_heads)], None)
            acc_ref[t] += pv
            return j - 1, store_runs(t, runs)

        lax.while_loop(cond, body, (j_first, top_run))

    @pl.when(step == 0)
    def _():
        pv, _ = diagonal_tile(0, 0)
        acc_ref[0] = pv
        tops = [None]
        for t in range(1, Q_TILES):
            pv, runs = diagonal_and_previous_tile(t, t)
            acc_ref[t] = pv
            tops.append(store_runs(t, runs) if t >= 2 else None)
        for t in range(Q_TILES):
            if t >= 2:
                earlier_tiles(t, t - 2, tops[t])
            o_ref[t * tq:(t + 1) * tq, :] = acc_ref[t].astype(o_ref.dtype)

    @pl.when(step > 0)
    def _():
        first = step * Q_TILES
        tops = []
        for t in range(Q_TILES):
            pv, runs = diagonal_and_previous_tile(t, first + t)
            acc_ref[t] = pv
            tops.append(store_runs(t, runs))
        for t in range(Q_TILES):
            earlier_tiles(t, first + t - 2, tops[t])
            o_ref[t * tq:(t + 1) * tq, :] = acc_ref[t].astype(o_ref.dtype)


def sb_attention(qkv, *, batch, head_dim, tq=256, width=HEAD_BLOCK):
    s = qkv.shape[0]
    d = qkv.shape[1] // (3 * batch)
    n_blk = d // width
    tq = min(tq, s // Q_TILES)
    rows = tq * Q_TILES
    r = lax.broadcasted_iota(jnp.int32, (2 * tq, tq), 0) % tq
    c = lax.broadcasted_iota(jnp.int32, (2 * tq, tq), 1)
    tri = jnp.where(r >= c, -1.0, 0.0).astype(_BF16)
    return pl.pallas_call(
        functools.partial(_sb_attn_kernel, tq=tq, head_dim=head_dim),
        grid=(batch, n_blk, s // rows),
        in_specs=[pl.BlockSpec((rows, width), lambda b, p, i: (i, b * 3 * n_blk + p)),
                  pl.BlockSpec((s, width), lambda b, p, i: (0, b * 3 * n_blk + n_blk + p)),
                  pl.BlockSpec((s, width), lambda b, p, i: (0, b * 3 * n_blk + 2 * n_blk + p)),
                  _resident((2 * tq, tq))],
        out_specs=pl.BlockSpec((rows, width), lambda b, p, i: (i, b * n_blk + p)),
        out_shape=jax.ShapeDtypeStruct((s, batch * d), _BF16),
        scratch_shapes=[pltpu.VMEM((Q_TILES, tq, width), _F32),
                        pltpu.VMEM((Q_TILES, width // head_dim, tq, 1), _F32)],
        compiler_params=pltpu.CompilerParams(
            dimension_semantics=("parallel", "parallel", "arbitrary"),
            vmem_limit_bytes=VMEM_LIMIT),
        name="sb_attention",
    )(qkv, qkv, qkv, tri)


def _gelu_tanh(x):
    c = math.sqrt(2.0 / math.pi)
    return 0.5 * x * (1.0 + jnp.tanh(c * (x + 0.044715 * (x * x * x))))


def _one_minus_exp2x(x):
    t = jnp.tanh(x)
    return (-2.0 * t) / (1.0 - t)


def _rglru_kernel(x_ref, g_ref, win_ref, cw_ref, cb_ref, wg_ref, ba_ref, bx_ref, lam_ref, y_ref,
                  xt_ref, yt_ref, xs_ref, a_ref, b_ref, h_ref, *, tc, nb, n_sub):
    d = x_ref.shape[1] // nb
    w = y_ref.shape[1] // nb
    ts = tc // n_sub
    rows = ts * nb
    tail = (RG_CONV - 1) * nb

    @pl.when(pl.program_id(0) == 0)
    def _():
        xs_ref[0:tail, :] = jnp.zeros((tail, w), _F32)
        h_ref[...] = jnp.zeros_like(h_ref)

    half_c = (-0.5 * RG_C) * _softplus(-lam_ref[...])
    h = h_ref[...]
    for sc in range(n_sub):
        r0 = sc * rows
        for bi in range(nb):
            for c in range(d // LANES):
                xt_ref[c, pl.ds(r0 + bi, ts, stride=nb), :] = (
                    x_ref[sc * ts:(sc + 1) * ts, bi * d + c * LANES:bi * d + (c + 1) * LANES])
        x_t = jnp.concatenate([xt_ref[c, r0:r0 + rows, :] for c in range(d // LANES)], axis=1)
        hn = _rms_norm(x_t, g_ref[...]).astype(_BF16)
        gate = _gelu_tanh(jnp.dot(hn, win_ref[:, 0:w], preferred_element_type=_F32))
        xs_ref[tail + r0:tail + r0 + rows, :] = jnp.dot(hn, win_ref[:, w:2 * w],
                                                        preferred_element_type=_F32)
        xc = cb_ref[...]
        for tap in range(RG_CONV):
            xc = xc + xs_ref[tap * nb + r0:tap * nb + r0 + rows, :] * cw_ref[tap:tap + 1, :]

        xcb = xc.astype(_BF16)
        for gi in range(w // GATE_GROUP):
            sl = slice(gi * GATE_GROUP, (gi + 1) * GATE_GROUP)
            pre = jnp.dot(xcb[:, sl], wg_ref[gi], preferred_element_type=_F32)
            t_r = jnp.tanh(pre[:, 0:GATE_GROUP] + ba_ref[:, sl])
            t_i = jnp.tanh(pre[:, GATE_GROUP:2 * GATE_GROUP] + bx_ref[:, sl])
            log_a = t_r * half_c[:, sl] + half_c[:, sl]
            ig = 0.5 * t_i + 0.5
            a_ref[r0:r0 + rows, sl] = jnp.exp(log_a)
            b_ref[r0:r0 + rows, sl] = _sqrt_nonneg(_one_minus_exp2x(log_a)) * (ig * xc[:, sl])

        for t in range(sc * ts, (sc + 1) * ts):
            sl = slice(t * nb, (t + 1) * nb)
            h = a_ref[sl, :] * h + b_ref[sl, :]
            b_ref[sl, :] = h

        y_t = b_ref[r0:r0 + rows, :] * gate
        for c in range(w // LANES):
            yt_ref[c, r0:r0 + rows, :] = y_t[:, c * LANES:(c + 1) * LANES]
        for bi in range(nb):
            for c in range(w // LANES):
                y_ref[sc * ts:(sc + 1) * ts, bi * w + c * LANES:bi * w + (c + 1) * LANES] = (
                    yt_ref[c, pl.ds(r0 + bi, ts, stride=nb), :])

    h_ref[...] = h
    xs_ref[0:tail, :] = xs_ref[tc * nb:tc * nb + tail, :]


def rglru_mixer(x, g, w_in_stack, layer, conv_w, conv_b, w_gates, b_a, b_x, lam, *, batch, tc=32,
                n_sub=2):
    seq = x.shape[0]
    d = x.shape[1] // batch
    assert batch % SUBLANES == 0, "time-step slabs must be whole sublane tiles"
    w = w_in_stack.shape[2] // 2
    tc = min(tc, seq)
    assert tc >= RG_CONV - 1 and tc % n_sub == 0
    row = lambda v: v.reshape(1, -1)
    return pl.pallas_call(
        functools.partial(_rglru_kernel, tc=tc, nb=batch, n_sub=n_sub),
        grid=(seq // tc,),
        in_specs=[pl.BlockSpec((tc, batch * d), lambda i: (i, 0)),
                  _resident((1, d)),
                  _resident_layer(w_in_stack.shape, layer),
                  _resident((RG_CONV, w)),
                  _resident((1, w)),
                  _resident(w_gates.shape),
                  _resident((1, w)),
                  _resident((1, w)),
                  _resident((1, w))],
        out_specs=pl.BlockSpec((tc, batch * w), lambda i: (i, 0)),
        out_shape=jax.ShapeDtypeStruct((seq, batch * w), _F32),
        scratch_shapes=[pltpu.VMEM((d // LANES, tc * batch, LANES), _F32),
                        pltpu.VMEM((w // LANES, tc * batch, LANES), _F32),
                        pltpu.VMEM(((tc + RG_CONV - 1) * batch, w), _F32),
                        pltpu.VMEM((tc * batch, w), _F32),
                        pltpu.VMEM((tc * batch, w), _F32),
                        pltpu.VMEM((batch, w), _F32)],
        compiler_params=pltpu.CompilerParams(dimension_semantics=("arbitrary",),
                                             vmem_limit_bytes=VMEM_LIMIT),
        name="rglru_mixer",
    )(x, row(g), w_in_stack, conv_w, row(conv_b), w_gates, row(b_a), row(b_x), row(lam))


def _proj_mlp_kernel(x_ref, y_ref, wo_ref, g_ref, w1_ref, w2_ref, gf_ref, o_ref, x1_ref, h_ref,
                     *, ff_chunk, final_norm):
    x1_ref[...] = x_ref[...] + jnp.dot(y_ref[...].astype(_BF16), wo_ref[...],
                                       preferred_element_type=_F32)
    h_ref[...] = _rms_norm(x1_ref[...], g_ref[...]).astype(_BF16)
    d_ff = w1_ref.shape[1]
    for c in range(0, d_ff, ff_chunk):
        u = jnp.maximum(jnp.dot(h_ref[...], w1_ref[:, c:c + ff_chunk],
                                preferred_element_type=_F32), 0.0)
        x1_ref[...] += jnp.dot((u * u).astype(_BF16), w2_ref[c:c + ff_chunk, :],
                               preferred_element_type=_F32)
    out = x1_ref[...]
    if final_norm:
        out = _rms_norm(out, gf_ref[...])
    o_ref[...] = out


def proj_mlp(x, y, w_o_stack, o_layer, g, w1_stack, w2_stack, layer, g_final, *, batch, seq,
             in_layout, out_layout, final_norm, tm=512, ff_chunk=1024):
    _, d, d_ff = w1_stack.shape
    dy = w_o_stack.shape[1]
    tm = min(tm, seq)
    ff_chunk = min(ff_chunk, d_ff)
    return pl.pallas_call(
        functools.partial(_proj_mlp_kernel, ff_chunk=ff_chunk, final_norm=final_norm),
        grid=(batch, seq // tm),
        in_specs=[_token_spec(in_layout, tm, d, seq),
                  _token_spec("sb", tm, dy, seq),
                  _resident_layer(w_o_stack.shape, o_layer),
                  _resident((1, d)),
                  _resident_layer(w1_stack.shape, layer),
                  _resident_layer(w2_stack.shape, layer),
                  _resident((1, d))],
        out_specs=_token_spec(out_layout, tm, d, seq),
        out_shape=jax.ShapeDtypeStruct(_token_shape(out_layout, batch, seq, d), _F32),
        scratch_shapes=[pltpu.VMEM((tm, d), _F32),
                        pltpu.VMEM((tm, d), _BF16)],
        compiler_params=pltpu.CompilerParams(dimension_semantics=("parallel", "parallel"),
                                             vmem_limit_bytes=VMEM_LIMIT),
        name="proj_mlp",
    )(x, y, w_o_stack, g.reshape(1, d), w1_stack, w2_stack, g_final.reshape(1, d))


def _block_diag_groups(w_blocks):
    nb, bd, _ = w_blocks.shape
    per = GATE_GROUP // bd
    wg = w_blocks.reshape(nb // per, per, bd, bd)
    eye = jnp.eye(per, dtype=w_blocks.dtype)
    return jnp.einsum("girc,ij->girjc", wg, eye).reshape(nb // per, GATE_GROUP, GATE_GROUP)


def kernel(x, norm_mix, norm_mlp, mlp_w1, mlp_w2, sb_w_qkv, sb_w_o, rg_w_in, rg_conv_w, rg_conv_b,
           rg_w_a, rg_b_a, rg_w_x, rg_b_x, rg_lambda, rg_w_o, norm_final):
    b, s, d = x.shape
    depth = norm_mix.shape[0]
    head_dim = d // SB_HEADS
    q_scale = jnp.concatenate([jnp.full((d,), math.log2(math.e) * head_dim ** -0.5, _F32),
                               jnp.ones((2 * d,), _F32)])

    w_qkv = (sb_w_qkv * q_scale).astype(_BF16)
    sb_w_o, rg_w_in, rg_w_o = sb_w_o.astype(_BF16), rg_w_in.astype(_BF16), rg_w_o.astype(_BF16)
    mlp_w1, mlp_w2 = mlp_w1.astype(_BF16), mlp_w2.astype(_BF16)

    cur, layout = x.reshape(b * s, d), "bs"
    ia = ib = 0
    for layer in range(depth):
        if layer % 2 == 0:
            qkv = norm_proj(cur, norm_mix[layer], w_qkv, ia, batch=b, seq=s, in_layout=layout)
            y = sb_attention(qkv, batch=b, head_dim=head_dim)
            w_o, o_layer = sb_w_o, ia
            ia += 1
        else:
            assert layout == "sb"
            w_gates = (0.5 * jnp.concatenate([_block_diag_groups(rg_w_a[ib]),
                                              _block_diag_groups(rg_w_x[ib])], axis=-1)
                       ).astype(_BF16)
            y = rglru_mixer(cur, norm_mix[layer], rg_w_in, ib, rg_conv_w[ib], rg_conv_b[ib],
                            w_gates, 0.5 * rg_b_a[ib], 0.5 * rg_b_x[ib], rg_lambda[ib], batch=b)
            w_o, o_layer = rg_w_o, ib
            ib += 1
        out_layout = "bs" if layer == depth - 1 else "sb"
        cur = proj_mlp(cur, y, w_o, o_layer, norm_mlp[layer], mlp_w1, mlp_w2, layer, norm_final,
                       batch=b, seq=s, in_layout=layout, out_layout=out_layout,
                       final_norm=(layer == depth - 1))
        layout = out_layout
    return cur.reshape(b, s, d)
```

```python
import functools
import math

import jax
import jax.numpy as jnp
from jax import lax
from jax.experimental import pallas as pl
from jax.experimental.pallas import tpu as pltpu

EPS = 1e-6
SB_HEADS = 16
RG_CONV = 4
RG_C = 8.0
GATE_GROUP = 256
HEAD_BLOCK = 256
Q_TILES = 8
SUBLANES = 8
LANES = 128
VMEM_LIMIT = 56 * 1024 * 1024

_BF16 = jnp.bfloat16
_F32 = jnp.float32


def _rms_norm(x, g):
    ms = jnp.mean(x * x, axis=-1, keepdims=True)
    return x * lax.rsqrt(ms + EPS) * g


def _softplus(z):
    return jnp.maximum(z, 0.0) + jnp.log(1.0 + jnp.exp(-jnp.abs(z)))


def _sqrt_nonneg(x):
    return jnp.where(x > 0.0, x * lax.rsqrt(x), 0.0)


def _resident(shape):
    n = len(shape)
    return pl.BlockSpec(shape, lambda *_: (0,) * n, pipeline_mode=pl.Buffered(1))


def _resident_layer(stack_shape, layer):
    return pl.BlockSpec((None,) + tuple(stack_shape[1:]), lambda *_: (layer, 0, 0),
                        pipeline_mode=pl.Buffered(1))


def _token_shape(layout, batch, seq, d):
    return (batch * seq, d) if layout == "bs" else (seq, batch * d)


def _token_spec(layout, tm, d, seq):
    if layout == "bs":
        return pl.BlockSpec((tm, d), lambda b, j: (b * (seq // tm) + j, 0))
    return pl.BlockSpec((tm, d), lambda b, j: (j, b))


def _norm_proj_kernel(x_ref, g_ref, w_ref, o_ref, h_ref, *, n_chunk):
    h_ref[...] = _rms_norm(x_ref[...], g_ref[...]).astype(_BF16)
    n = w_ref.shape[1]
    for c in range(0, n, n_chunk):
        o_ref[:, c:c + n_chunk] = jnp.dot(
            h_ref[...], w_ref[:, c:c + n_chunk], preferred_element_type=_F32).astype(o_ref.dtype)


def norm_proj(x, g, w_stack, layer, *, batch, seq, in_layout, tm=1024, n_chunk=512):
    _, d, n = w_stack.shape
    tm = min(tm, seq)
    n_chunk = min(n_chunk, n)
    return pl.pallas_call(
        functools.partial(_norm_proj_kernel, n_chunk=n_chunk),
        grid=(batch, seq // tm),
        in_specs=[_token_spec(in_layout, tm, d, seq),
                  _resident((1, d)),
                  _resident_layer(w_stack.shape, layer)],
        out_specs=_token_spec("sb", tm, n, seq),
        out_shape=jax.ShapeDtypeStruct(_token_shape("sb", batch, seq, n), _BF16),
        scratch_shapes=[pltpu.VMEM((tm, d), _BF16)],
        compiler_params=pltpu.CompilerParams(dimension_semantics=("parallel", "parallel"),
                                             vmem_limit_bytes=VMEM_LIMIT),
        name="norm_proj",
    )(x, g.reshape(1, d), w_stack)


def _softplus2(z):
    return jnp.maximum(z, 0.0) + jnp.log2(1.0 + jnp.exp2(-jnp.abs(z)))


_EXP2_ZERO_BELOW = -152.0


def _sb_qk(q_heads, k_t):
    return [lax.dot_general(q_h, k_t, (((1,), (1,)), ((), ())), preferred_element_type=_F32)
            for q_h in q_heads]


def _sb_keep(zs, mask):
    hilos = []
    for z in zs:
        sp = _softplus2(z)
        if mask is not None:
            sp = jnp.where(mask, sp, 0.0)
        hi = sp.astype(_BF16)
        lo = (sp - hi.astype(_F32)).astype(_BF16)
        hilos.append(jnp.concatenate([hi, lo], axis=1))
    return hilos


def _sb_cumsum(hilos, tri):
    return [jnp.dot(hilo, tri, preferred_element_type=_F32) for hilo in hilos]


def _sb_values(zs, incls, v_heads, runs, mask, pv_dots=1):
    ps, new_runs = [], []
    for h, (z, incl) in enumerate(zip(zs, incls)):
        e = z + incl
        tot = incl[:, 0:1]
        if runs is not None:
            e = e + runs[h]
            tot = tot + runs[h]
        p = jnp.exp2(e)
        if mask is not None:
            p = jnp.where(mask, p, 0.0)
        ps.append(p.astype(_BF16))
        new_runs.append(tot)
    per = len(ps) // pv_dots
    rows = per * v_heads.shape[0] // len(ps)
    pv = None
    for c in range(pv_dots):
        part = jnp.dot(jnp.concatenate(ps[c * per:(c + 1) * per], axis=1),
                       v_heads[c * rows:(c + 1) * rows], preferred_element_type=_F32)
        pv = part if pv is None else pv + part
    return pv, new_runs


def _sb_attn_kernel(q_ref, k_ref, v_ref, tri_ref, o_ref, acc_ref, run_ref, *, tq, head_dim,
                    n_steps):
    step = pl.program_id(2)
    width = q_ref.shape[1]
    n_heads = width // head_dim
    head_of_lane = lax.broadcasted_iota(jnp.int32, (tq, width), 1) // head_dim
    zero = jnp.zeros((tq, width), q_ref.dtype)
    tri = tri_ref[...]
    row = lax.broadcasted_iota(jnp.int32, (tq, tq), 0)
    col = lax.broadcasted_iota(jnp.int32, (tq, tq), 1)
    causal = col < row

    def q_heads(t):
        q = q_ref[t * tq:(t + 1) * tq, :]
        return [jnp.where(head_of_lane == h, q, zero) for h in range(n_heads)]

    def k_tile(j):
        return k_ref[pl.ds(pl.multiple_of(j * tq, tq), tq), :]

    def v_tile(j):
        v_t = v_ref[pl.ds(pl.multiple_of(j * tq, tq), tq), :]
        return jnp.concatenate([jnp.where(head_of_lane == h, v_t, zero)
                                for h in range(n_heads)], axis=0)

    def diagonal_tile(t, j):
        qh = q_heads(t)
        zs = _sb_qk(qh, k_tile(j))
        incls = _sb_cumsum(_sb_keep(zs, causal), tri)
        return _sb_values(zs, incls, v_tile(j), None, causal, n_heads)

    def diagonal_and_previous_tile(t, j):
        qh = q_heads(t)
        zs_d, zs_p = _sb_qk(qh, k_tile(j)), _sb_qk(qh, k_tile(j - 1))
        incls_d = _sb_cumsum(_sb_keep(zs_d, causal), tri)
        incls_p = _sb_cumsum(_sb_keep(zs_p, None), tri)
        pv_d, runs = _sb_values(zs_d, incls_d, v_tile(j), None, causal, n_heads)
        pv_p, runs = _sb_values(zs_p, incls_p, v_tile(j - 1), runs, None, n_heads)
        return pv_d + pv_p, runs

    def store_runs(t, runs):
        top = runs[0]
        for h in range(n_heads):
            run_ref[t, h] = runs[h]
            top = jnp.maximum(top, runs[h])
        return jnp.max(top)

    def earlier_tiles(t, j_first, top_run):
        def cond(carry):
            j, top_run = carry
            return jnp.logical_and(j >= 0, top_run >= _EXP2_ZERO_BELOW)

        def body(carry):
            j, _ = carry
            zs = _sb_qk(q_heads(t), k_tile(j))
            incls = _sb_cumsum(_sb_keep(zs, None), tri)
            pv, runs = _sb_values(zs, incls, v_tile(j),
                                  [run_ref[t, h] for h in range(n_heads)], None)
            acc_ref[t] += pv
            return j - 1, store_runs(t, runs)

        lax.while_loop(cond, body, (j_first, top_run))

    def first_step():
        pv, _ = diagonal_tile(0, 0)
        acc_ref[0] = pv
        tops = [None]
        for t in range(1, Q_TILES):
            pv, runs = diagonal_and_previous_tile(t, t)
            acc_ref[t] = pv
            tops.append(store_runs(t, runs) if t >= 2 else None)
        for t in range(Q_TILES):
            if t >= 2:
                earlier_tiles(t, t - 2, tops[t])
            o_ref[t * tq:(t + 1) * tq, :] = acc_ref[t].astype(o_ref.dtype)

    if n_steps == 1:
        first_step()
        return
    pl.when(step == 0)(first_step)

    @pl.when(step > 0)
    def _():
        first = step * Q_TILES
        tops = []
        for t in range(Q_TILES):
            pv, runs = diagonal_and_previous_tile(t, first + t)
            acc_ref[t] = pv
            tops.append(store_runs(t, runs))
        for t in range(Q_TILES):
            earlier_tiles(t, first + t - 2, tops[t])
            o_ref[t * tq:(t + 1) * tq, :] = acc_ref[t].astype(o_ref.dtype)


def sb_attention(qkv, *, batch, head_dim, tq=256, width=HEAD_BLOCK):
    s = qkv.shape[0]
    d = qkv.shape[1] // (3 * batch)
    n_blk = d // width
    tq = min(tq, s // Q_TILES)
    rows = tq * Q_TILES
    r = lax.broadcasted_iota(jnp.int32, (2 * tq, tq), 0) % tq
    c = lax.broadcasted_iota(jnp.int32, (2 * tq, tq), 1)
    tri = jnp.where(r >= c, -1.0, 0.0).astype(_BF16)
    return pl.pallas_call(
        functools.partial(_sb_attn_kernel, tq=tq, head_dim=head_dim, n_steps=s // rows),
        grid=(batch, n_blk, s // rows),
        in_specs=[pl.BlockSpec((rows, width), lambda b, p, i: (i, b * 3 * n_blk + p)),
                  pl.BlockSpec((s, width), lambda b, p, i: (0, b * 3 * n_blk + n_blk + p)),
                  pl.BlockSpec((s, width), lambda b, p, i: (0, b * 3 * n_blk + 2 * n_blk + p)),
                  _resident((2 * tq, tq))],
        out_specs=pl.BlockSpec((rows, width), lambda b, p, i: (i, b * n_blk + p)),
        out_shape=jax.ShapeDtypeStruct((s, batch * d), _BF16),
        scratch_shapes=[pltpu.VMEM((Q_TILES, tq, width), _F32),
                        pltpu.VMEM((Q_TILES, width // head_dim, tq, 1), _F32)],
        compiler_params=pltpu.CompilerParams(
            dimension_semantics=("parallel", "parallel", "arbitrary"),
            vmem_limit_bytes=VMEM_LIMIT),
        name="sb_attention",
    )(qkv, qkv, qkv, tri)


def _gelu_tanh(x):
    c = math.sqrt(2.0 / math.pi)
    return 0.5 * x * (1.0 + jnp.tanh(c * (x + 0.044715 * (x * x * x))))


def _one_minus_exp2x(x):
    t = jnp.tanh(x)
    return (-2.0 * t) / (1.0 - t)


def _rglru_kernel(x_ref, g_ref, win_ref, cw_ref, cb_ref, wg_ref, ba_ref, bx_ref, lam_ref, y_ref,
                  xt_ref, yt_ref, xs_ref, a_ref, b_ref, h_ref, *, tc, nb, n_sub):
    d = x_ref.shape[1] // nb
    w = y_ref.shape[1] // nb
    ts = tc // n_sub
    rows = ts * nb
    tail = (RG_CONV - 1) * nb

    @pl.when(pl.program_id(0) == 0)
    def _():
        xs_ref[0:tail, :] = jnp.zeros((tail, w), _F32)
        h_ref[...] = jnp.zeros_like(h_ref)

    half_c = (-0.5 * RG_C) * _softplus(-lam_ref[...])
    h = h_ref[...]
    for sc in range(n_sub):
        r0 = sc * rows
        for bi in range(nb):
            for c in range(d // LANES):
                xt_ref[c, pl.ds(r0 + bi, ts, stride=nb), :] = (
                    x_ref[sc * ts:(sc + 1) * ts, bi * d + c * LANES:bi * d + (c + 1) * LANES])
        x_t = jnp.concatenate([xt_ref[c, r0:r0 + rows, :] for c in range(d // LANES)], axis=1)
        hn = _rms_norm(x_t, g_ref[...]).astype(_BF16)
        gate = _gelu_tanh(jnp.dot(hn, win_ref[:, 0:w], preferred_element_type=_F32))
        xs_ref[tail + r0:tail + r0 + rows, :] = jnp.dot(hn, win_ref[:, w:2 * w],
                                                        preferred_element_type=_F32)
        xc = cb_ref[...]
        for tap in range(RG_CONV):
            xc = xc + xs_ref[tap * nb + r0:tap * nb + r0 + rows, :] * cw_ref[tap:tap + 1, :]

        xcb = xc.astype(_BF16)
        for gi in range(w // GATE_GROUP):
            sl = slice(gi * GATE_GROUP, (gi + 1) * GATE_GROUP)
            pre = jnp.dot(xcb[:, sl], wg_ref[gi], preferred_element_type=_F32)
            t_r = jnp.tanh(pre[:, 0:GATE_GROUP] + ba_ref[:, sl])
            t_i = jnp.tanh(pre[:, GATE_GROUP:2 * GATE_GROUP] + bx_ref[:, sl])
            log_a = t_r * half_c[:, sl] + half_c[:, sl]
            ig = 0.5 * t_i + 0.5
            a_ref[r0:r0 + rows, sl] = jnp.exp(log_a)
            b_ref[r0:r0 + rows, sl] = _sqrt_nonneg(_one_minus_exp2x(log_a)) * (ig * xc[:, sl])

        for t in range(sc * ts, (sc + 1) * ts):
            sl = slice(t * nb, (t + 1) * nb)
            h = a_ref[sl, :] * h + b_ref[sl, :]
            b_ref[sl, :] = h

        y_t = b_ref[r0:r0 + rows, :] * gate
        for c in range(w // LANES):
            yt_ref[c, r0:r0 + rows, :] = y_t[:, c * LANES:(c + 1) * LANES]
        for bi in range(nb):
            for c in range(w // LANES):
                y_ref[sc * ts:(sc + 1) * ts, bi * w + c * LANES:bi * w + (c + 1) * LANES] = (
                    yt_ref[c, pl.ds(r0 + bi, ts, stride=nb), :])

    h_ref[...] = h
    xs_ref[0:tail, :] = xs_ref[tc * nb:tc * nb + tail, :]


def rglru_mixer(x, g, w_in_stack, layer, conv_w, conv_b, w_gates, b_a, b_x, lam, *, batch, tc=32,
                n_sub=2):
    seq = x.shape[0]
    d = x.shape[1] // batch
    assert batch % SUBLANES == 0, "time-step slabs must be whole sublane tiles"
    w = w_in_stack.shape[2] // 2
    tc = min(tc, seq)
    assert tc >= RG_CONV - 1 and tc % n_sub == 0
    row = lambda v: v.reshape(1, -1)
    return pl.pallas_call(
        functools.partial(_rglru_kernel, tc=tc, nb=batch, n_sub=n_sub),
        grid=(seq // tc,),
        in_specs=[pl.BlockSpec((tc, batch * d), lambda i: (i, 0)),
                  _resident((1, d)),
                  _resident_layer(w_in_stack.shape, layer),
                  _resident((RG_CONV, w)),
                  _resident((1, w)),
                  _resident(w_gates.shape),
                  _resident((1, w)),
                  _resident((1, w)),
                  _resident((1, w))],
        out_specs=pl.BlockSpec((tc, batch * w), lambda i: (i, 0)),
        out_shape=jax.ShapeDtypeStruct((seq, batch * w), _F32),
        scratch_shapes=[pltpu.VMEM((d // LANES, tc * batch, LANES), _F32),
                        pltpu.VMEM((w // LANES, tc * batch, LANES), _F32),
                        pltpu.VMEM(((tc + RG_CONV - 1) * batch, w), _F32),
                        pltpu.VMEM((tc * batch, w), _F32),
                        pltpu.VMEM((tc * batch, w), _F32),
                        pltpu.VMEM((batch, w), _F32)],
        compiler_params=pltpu.CompilerParams(dimension_semantics=("arbitrary",),
                                             vmem_limit_bytes=VMEM_LIMIT),
        name="rglru_mixer",
    )(x, row(g), w_in_stack, conv_w, row(conv_b), w_gates, row(b_a), row(b_x), row(lam))


def _proj_mlp_kernel(x_ref, y_ref, wo_ref, g_ref, w1_ref, w2_ref, gf_ref, o_ref, x1_ref, h_ref,
                     *, ff_chunk, final_norm):
    x1_ref[...] = x_ref[...] + jnp.dot(y_ref[...].astype(_BF16), wo_ref[...],
                                       preferred_element_type=_F32)
    h_ref[...] = _rms_norm(x1_ref[...], g_ref[...]).astype(_BF16)
    d_ff = w1_ref.shape[1]
    for c in range(0, d_ff, ff_chunk):
        u = jnp.maximum(jnp.dot(h_ref[...], w1_ref[:, c:c + ff_chunk],
                                preferred_element_type=_F32), 0.0)
        x1_ref[...] += jnp.dot((u * u).astype(_BF16), w2_ref[c:c + ff_chunk, :],
                               preferred_element_type=_F32)
    out = x1_ref[...]
    if final_norm:
        out = _rms_norm(out, gf_ref[...])
    o_ref[...] = out


def proj_mlp(x, y, w_o_stack, o_layer, g, w1_stack, w2_stack, layer, g_final, *, batch, seq,
             in_layout, out_layout, final_norm, tm=1024, ff_chunk=1024):
    _, d, d_ff = w1_stack.shape
    dy = w_o_stack.shape[1]
    tm = min(tm, seq)
    ff_chunk = min(ff_chunk, d_ff)
    return pl.pallas_call(
        functools.partial(_proj_mlp_kernel, ff_chunk=ff_chunk, final_norm=final_norm),
        grid=(batch, seq // tm),
        in_specs=[_token_spec(in_layout, tm, d, seq),
                  _token_spec("sb", tm, dy, seq),
                  _resident_layer(w_o_stack.shape, o_layer),
                  _resident((1, d)),
                  _resident_layer(w1_stack.shape, layer),
                  _resident_layer(w2_stack.shape, layer),
                  _resident((1, d))],
        out_specs=_token_spec(out_layout, tm, d, seq),
        out_shape=jax.ShapeDtypeStruct(_token_shape(out_layout, batch, seq, d), _F32),
        scratch_shapes=[pltpu.VMEM((tm, d), _F32),
                        pltpu.VMEM((tm, d), _BF16)],
        compiler_params=pltpu.CompilerParams(dimension_semantics=("parallel", "parallel"),
                                             vmem_limit_bytes=VMEM_LIMIT),
        name="proj_mlp",
    )(x, y, w_o_stack, g.reshape(1, d), w1_stack, w2_stack, g_final.reshape(1, d))


def _block_diag_groups(w_blocks):
    nb, bd, _ = w_blocks.shape
    per = GATE_GROUP // bd
    wg = w_blocks.reshape(nb // per, per, bd, bd)
    eye = jnp.eye(per, dtype=w_blocks.dtype)
    return jnp.einsum("girc,ij->girjc", wg, eye).reshape(nb // per, GATE_GROUP, GATE_GROUP)


def kernel(x, norm_mix, norm_mlp, mlp_w1, mlp_w2, sb_w_qkv, sb_w_o, rg_w_in, rg_conv_w, rg_conv_b,
           rg_w_a, rg_b_a, rg_w_x, rg_b_x, rg_lambda, rg_w_o, norm_final):
    b, s, d = x.shape
    depth = norm_mix.shape[0]
    head_dim = d // SB_HEADS
    q_scale = jnp.concatenate([jnp.full((d,), math.log2(math.e) * head_dim ** -0.5, _F32),
                               jnp.ones((2 * d,), _F32)])

    w_qkv = (sb_w_qkv * q_scale).astype(_BF16)
    sb_w_o, rg_w_in, rg_w_o = sb_w_o.astype(_BF16), rg_w_in.astype(_BF16), rg_w_o.astype(_BF16)
    mlp_w1, mlp_w2 = mlp_w1.astype(_BF16), mlp_w2.astype(_BF16)

    cur, layout = x.reshape(b * s, d), "bs"
    ia = ib = 0
    for layer in range(depth):
        if layer % 2 == 0:
            qkv = norm_proj(cur, norm_mix[layer], w_qkv, ia, batch=b, seq=s, in_layout=layout)
            y = sb_attention(qkv, batch=b, head_dim=head_dim)
            w_o, o_layer = sb_w_o, ia
            ia += 1
        else:
            assert layout == "sb"
            w_gates = (0.5 * jnp.concatenate([_block_diag_groups(rg_w_a[ib]),
                                              _block_diag_groups(rg_w_x[ib])], axis=-1)
                       ).astype(_BF16)
            y = rglru_mixer(cur, norm_mix[layer], rg_w_in, ib, rg_conv_w[ib], rg_conv_b[ib],
                            w_gates, 0.5 * rg_b_a[ib], 0.5 * rg_b_x[ib], rg_lambda[ib], batch=b)
            w_o, o_layer = rg_w_o, ib
            ib += 1
        out_layout = "bs" if layer == depth - 1 else "sb"
        cur = proj_mlp(cur, y, w_o, o_layer, norm_mlp[layer], mlp_w1, mlp_w2, layer, norm_final,
                       batch=b, seq=s, in_layout=layout, out_layout=out_layout,
                       final_norm=(layer == depth - 1))
        layout = out_layout
    return cur.reshape(b, s, d)
```

```python
import functools
import math

import jax
import jax.numpy as jnp
from jax import lax
from jax.experimental import pallas as pl
from jax.experimental.pallas import tpu as pltpu

EPS = 1e-6
SB_HEADS = 16
RG_CONV = 4
RG_C = 8.0
GATE_GROUP = 256
HEAD_BLOCK = 256
Q_TILES = 8
SUBLANES = 8
LANES = 128
VMEM_LIMIT = 56 * 1024 * 1024

_BF16 = jnp.bfloat16
_F32 = jnp.float32


def _rms_norm(x, g):
    ms = jnp.mean(x * x, axis=-1, keepdims=True)
    return x * lax.rsqrt(ms + EPS) * g


def _softplus(z):
    return jnp.maximum(z, 0.0) + jnp.log(1.0 + jnp.exp(-jnp.abs(z)))


def _sqrt_nonneg(x):
    return jnp.where(x > 0.0, x * lax.rsqrt(x), 0.0)


def _resident(shape):
    n = len(shape)
    return pl.BlockSpec(shape, lambda *_: (0,) * n, pipeline_mode=pl.Buffered(1))


def _resident_layer(stack_shape, layer):
    return pl.BlockSpec((None,) + tuple(stack_shape[1:]), lambda *_: (layer, 0, 0),
                        pipeline_mode=pl.Buffered(1))


def _token_shape(layout, batch, seq, d):
    return (batch * seq, d) if layout == "bs" else (seq, batch * d)


def _token_spec(layout, tm, d, seq):
    if layout == "bs":
        return pl.BlockSpec((tm, d), lambda b, j: (b * (seq // tm) + j, 0))
    return pl.BlockSpec((tm, d), lambda b, j: (j, b))


def _norm_proj_kernel(x_ref, g_ref, w_ref, o_ref, h_ref, *, n_chunk):
    h_ref[...] = _rms_norm(x_ref[...], g_ref[...]).astype(_BF16)
    n = w_ref.shape[1]
    for c in range(0, n, n_chunk):
        o_ref[:, c:c + n_chunk] = jnp.dot(
            h_ref[...], w_ref[:, c:c + n_chunk], preferred_element_type=_F32).astype(o_ref.dtype)


def norm_proj(x, g, w_stack, layer, *, batch, seq, in_layout, tm=1024, n_chunk=512):
    _, d, n = w_stack.shape
    tm = min(tm, seq)
    n_chunk = min(n_chunk, n)
    return pl.pallas_call(
        functools.partial(_norm_proj_kernel, n_chunk=n_chunk),
        grid=(batch, seq // tm),
        in_specs=[_token_spec(in_layout, tm, d, seq),
                  _resident((1, d)),
                  _resident_layer(w_stack.shape, layer)],
        out_specs=_token_spec("sb", tm, n, seq),
        out_shape=jax.ShapeDtypeStruct(_token_shape("sb", batch, seq, n), _BF16),
        scratch_shapes=[pltpu.VMEM((tm, d), _BF16)],
        compiler_params=pltpu.CompilerParams(dimension_semantics=("parallel", "parallel"),
                                             vmem_limit_bytes=VMEM_LIMIT),
        name="norm_proj",
    )(x, g.reshape(1, d), w_stack)


def _softplus2(z):
    return jnp.maximum(z, 0.0) + jnp.log2(1.0 + jnp.exp2(-jnp.abs(z)))


_EXP2_ZERO_BELOW = -152.0


def _sb_qk(q_heads, k_t):
    return [lax.dot_general(q_h, k_t, (((1,), (1,)), ((), ())), preferred_element_type=_F32)
            for q_h in q_heads]


def _sb_keep(zs, mask):
    keeps = []
    for z in zs:
        sp = _softplus2(z)
        if mask is not None:
            sp = jnp.where(mask, sp, 0.0)
        keeps.append((sp.astype(_BF16), z - sp, sp[:, 0:1]))
    return keeps


def _sb_cumsum(keeps, tri):
    return [jnp.dot(sp, tri, preferred_element_type=_F32) for sp, _, _ in keeps]


def _sb_values(keeps, excls, v_heads, runs, mask, pv_dots=1):
    ps, new_runs = [], []
    for h, ((_, log_beta, sp0), excl) in enumerate(zip(keeps, excls)):
        e = log_beta + excl
        tot = excl[:, 0:1] - sp0
        if runs is not None:
            e = e + runs[h]
            tot = tot + runs[h]
        p = jnp.exp2(e)
        if mask is not None:
            p = jnp.where(mask, p, 0.0)
        ps.append(p.astype(_BF16))
        new_runs.append(tot)
    per = len(ps) // pv_dots
    rows = per * v_heads.shape[0] // len(ps)
    pv = None
    for c in range(pv_dots):
        part = jnp.dot(jnp.concatenate(ps[c * per:(c + 1) * per], axis=1),
                       v_heads[c * rows:(c + 1) * rows], preferred_element_type=_F32)
        pv = part if pv is None else pv + part
    return pv, new_runs


def _sb_attn_kernel(q_ref, k_ref, v_ref, tri_ref, o_ref, acc_ref, run_ref, *, tq, head_dim,
                    n_steps):
    step = pl.program_id(2)
    width = q_ref.shape[1]
    n_heads = width // head_dim
    head_of_lane = lax.broadcasted_iota(jnp.int32, (tq, width), 1) // head_dim
    zero = jnp.zeros((tq, width), q_ref.dtype)
    tri = tri_ref[...]
    row = lax.broadcasted_iota(jnp.int32, (tq, tq), 0)
    col = lax.broadcasted_iota(jnp.int32, (tq, tq), 1)
    causal = col < row

    def q_heads(t):
        q = q_ref[t * tq:(t + 1) * tq, :]
        return [jnp.where(head_of_lane == h, q, zero) for h in range(n_heads)]

    def k_tile(j):
        return k_ref[pl.ds(pl.multiple_of(j * tq, tq), tq), :]

    def v_tile(j):
        v_t = v_ref[pl.ds(pl.multiple_of(j * tq, tq), tq), :]
        return jnp.concatenate([jnp.where(head_of_lane == h, v_t, zero)
                                for h in range(n_heads)], axis=0)

    def diagonal_tile(t, j):
        qh = q_heads(t)
        keeps = _sb_keep(_sb_qk(qh, k_tile(j)), causal)
        return _sb_values(keeps, _sb_cumsum(keeps, tri), v_tile(j), None, causal, n_heads)

    def diagonal_and_previous_tile(t, j):
        qh = q_heads(t)
        keeps_d = _sb_keep(_sb_qk(qh, k_tile(j)), causal)
        keeps_p = _sb_keep(_sb_qk(qh, k_tile(j - 1)), None)
        excls_d, excls_p = _sb_cumsum(keeps_d, tri), _sb_cumsum(keeps_p, tri)
        pv_d, runs = _sb_values(keeps_d, excls_d, v_tile(j), None, causal, n_heads)
        pv_p, runs = _sb_values(keeps_p, excls_p, v_tile(j - 1), runs, None, n_heads)
        return pv_d + pv_p, runs

    def store_runs(t, runs):
        top = runs[0]
        for h in range(n_heads):
            run_ref[t, h] = runs[h]
            top = jnp.maximum(top, runs[h])
        return jnp.max(top)

    def earlier_tiles(t, j_first, top_run):
        def cond(carry):
            j, top_run = carry
            return jnp.logical_and(j >= 0, top_run >= _EXP2_ZERO_BELOW)

        def body(carry):
            j, _ = carry
            keeps = _sb_keep(_sb_qk(q_heads(t), k_tile(j)), None)
            pv, runs = _sb_values(keeps, _sb_cumsum(keeps, tri), v_tile(j),
                                  [run_ref[t, h] for h in range(n_heads)], None)
            acc_ref[t] += pv
            return j - 1, store_runs(t, runs)

        lax.while_loop(cond, body, (j_first, top_run))

    def first_step():
        pv, _ = diagonal_tile(0, 0)
        acc_ref[0] = pv
        tops = [None]
        for t in range(1, Q_TILES):
            pv, runs = diagonal_and_previous_tile(t, t)
            acc_ref[t] = pv
            tops.append(store_runs(t, runs) if t >= 2 else None)
        for t in range(Q_TILES):
            if t >= 2:
                earlier_tiles(t, t - 2, tops[t])
            o_ref[t * tq:(t + 1) * tq, :] = acc_ref[t].astype(o_ref.dtype)

    if n_steps == 1:
        first_step()
        return
    pl.when(step == 0)(first_step)

    @pl.when(step > 0)
    def _():
        first = step * Q_TILES
        tops = []
        for t in range(Q_TILES):
            pv, runs = diagonal_and_previous_tile(t, first + t)
            acc_ref[t] = pv
            tops.append(store_runs(t, runs))
        for t in range(Q_TILES):
            earlier_tiles(t, first + t - 2, tops[t])
            o_ref[t * tq:(t + 1) * tq, :] = acc_ref[t].astype(o_ref.dtype)


def sb_attention(qkv, *, batch, head_dim, tq=256, width=HEAD_BLOCK):
    s = qkv.shape[0]
    d = qkv.shape[1] // (3 * batch)
    n_blk = d // width
    tq = min(tq, s // Q_TILES)
    rows = tq * Q_TILES
    r = lax.broadcasted_iota(jnp.int32, (tq, tq), 0)
    c = lax.broadcasted_iota(jnp.int32, (tq, tq), 1)
    tri = jnp.where(r > c, -1.0, 0.0).astype(_BF16)
    return pl.pallas_call(
        functools.partial(_sb_attn_kernel, tq=tq, head_dim=head_dim, n_steps=s // rows),
        grid=(batch, n_blk, s // rows),
        in_specs=[pl.BlockSpec((rows, width), lambda b, p, i: (i, b * 3 * n_blk + p)),
                  pl.BlockSpec((s, width), lambda b, p, i: (0, b * 3 * n_blk + n_blk + p)),
                  pl.BlockSpec((s, width), lambda b, p, i: (0, b * 3 * n_blk + 2 * n_blk + p)),
                  _resident((tq, tq))],
        out_specs=pl.BlockSpec((rows, width), lambda b, p, i: (i, b * n_blk + p)),
        out_shape=jax.ShapeDtypeStruct((s, batch * d), _BF16),
        scratch_shapes=[pltpu.VMEM((Q_TILES, tq, width), _F32),
                        pltpu.VMEM((Q_TILES, width // head_dim, tq, 1), _F32)],
        compiler_params=pltpu.CompilerParams(
            dimension_semantics=("parallel", "parallel", "arbitrary"),
            vmem_limit_bytes=VMEM_LIMIT),
        name="sb_attention",
    )(qkv, qkv, qkv, tri)


def _gelu_tanh(x):
    c = math.sqrt(2.0 / math.pi)
    return 0.5 * x * (1.0 + jnp.tanh(c * (x + 0.044715 * (x * x * x))))


def _one_minus_exp2x(x):
    t = jnp.tanh(x)
    return (-2.0 * t) / (1.0 - t)


def _rglru_kernel(x_ref, g_ref, win_ref, cw_ref, cb_ref, wg_ref, ba_ref, bx_ref, lam_ref, y_ref,
                  xt_ref, yt_ref, xs_ref, a_ref, b_ref, h_ref, *, tc, nb, n_sub):
    d = x_ref.shape[1] // nb
    w = y_ref.shape[1] // nb
    ts = tc // n_sub
    rows = ts * nb
    tail = (RG_CONV - 1) * nb

    @pl.when(pl.program_id(0) == 0)
    def _():
        xs_ref[0:tail, :] = jnp.zeros((tail, w), _F32)
        h_ref[...] = jnp.zeros_like(h_ref)

    half_c = (-0.5 * RG_C) * _softplus(-lam_ref[...])
    h = h_ref[...]
    for sc in range(n_sub):
        r0 = sc * rows
        for bi in range(nb):
            for c in range(d // LANES):
                xt_ref[c, pl.ds(r0 + bi, ts, stride=nb), :] = (
                    x_ref[sc * ts:(sc + 1) * ts, bi * d + c * LANES:bi * d + (c + 1) * LANES])
        x_t = jnp.concatenate([xt_ref[c, r0:r0 + rows, :] for c in range(d // LANES)], axis=1)
        hn = _rms_norm(x_t, g_ref[...]).astype(_BF16)
        gate = _gelu_tanh(jnp.dot(hn, win_ref[:, 0:w], preferred_element_type=_F32))
        xs_ref[tail + r0:tail + r0 + rows, :] = jnp.dot(hn, win_ref[:, w:2 * w],
                                                        preferred_element_type=_F32)
        xc = cb_ref[...]
        for tap in range(RG_CONV):
            xc = xc + xs_ref[tap * nb + r0:tap * nb + r0 + rows, :] * cw_ref[tap:tap + 1, :]

        xcb = xc.astype(_BF16)
        for gi in range(w // GATE_GROUP):
            sl = slice(gi * GATE_GROUP, (gi + 1) * GATE_GROUP)
            pre = jnp.dot(xcb[:, sl], wg_ref[gi], preferred_element_type=_F32)
            t_r = jnp.tanh(pre[:, 0:GATE_GROUP] + ba_ref[:, sl])
            t_i = jnp.tanh(pre[:, GATE_GROUP:2 * GATE_GROUP] + bx_ref[:, sl])
            log_a = t_r * half_c[:, sl] + half_c[:, sl]
            ig = 0.5 * t_i + 0.5
            a_ref[r0:r0 + rows, sl] = jnp.exp(log_a)
            b_ref[r0:r0 + rows, sl] = _sqrt_nonneg(_one_minus_exp2x(log_a)) * (ig * xc[:, sl])

        for t in range(sc * ts, (sc + 1) * ts):
            sl = slice(t * nb, (t + 1) * nb)
            h = a_ref[sl, :] * h + b_ref[sl, :]
            b_ref[sl, :] = h

        y_t = b_ref[r0:r0 + rows, :] * gate
        for c in range(w // LANES):
            yt_ref[c, r0:r0 + rows, :] = y_t[:, c * LANES:(c + 1) * LANES]
        for bi in range(nb):
            for c in range(w // LANES):
                y_ref[sc * ts:(sc + 1) * ts, bi * w + c * LANES:bi * w + (c + 1) * LANES] = (
                    yt_ref[c, pl.ds(r0 + bi, ts, stride=nb), :])

    h_ref[...] = h
    xs_ref[0:tail, :] = xs_ref[tc * nb:tc * nb + tail, :]


def rglru_mixer(x, g, w_in_stack, layer, conv_w, conv_b, w_gates, b_a, b_x, lam, *, batch, tc=32,
                n_sub=2):
    seq = x.shape[0]
    d = x.shape[1] // batch
    assert batch % SUBLANES == 0, "time-step slabs must be whole sublane tiles"
    w = w_in_stack.shape[2] // 2
    tc = min(tc, seq)
    assert tc >= RG_CONV - 1 and tc % n_sub == 0
    row = lambda v: v.reshape(1, -1)
    return pl.pallas_call(
        functools.partial(_rglru_kernel, tc=tc, nb=batch, n_sub=n_sub),
        grid=(seq // tc,),
        in_specs=[pl.BlockSpec((tc, batch * d), lambda i: (i, 0)),
                  _resident((1, d)),
                  _resident_layer(w_in_stack.shape, layer),
                  _resident((RG_CONV, w)),
                  _resident((1, w)),
                  _resident(w_gates.shape),
                  _resident((1, w)),
                  _resident((1, w)),
                  _resident((1, w))],
        out_specs=pl.BlockSpec((tc, batch * w), lambda i: (i, 0)),
        out_shape=jax.ShapeDtypeStruct((seq, batch * w), _F32),
        scratch_shapes=[pltpu.VMEM((d // LANES, tc * batch, LANES), _F32),
                        pltpu.VMEM((w // LANES, tc * batch, LANES), _F32),
                        pltpu.VMEM(((tc + RG_CONV - 1) * batch, w), _F32),
                        pltpu.VMEM((tc * batch, w), _F32),
                        pltpu.VMEM((tc * batch, w), _F32),
                        pltpu.VMEM((batch, w), _F32)],
        compiler_params=pltpu.CompilerParams(dimension_semantics=("arbitrary",),
                                             vmem_limit_bytes=VMEM_LIMIT),
        name="rglru_mixer",
    )(x, row(g), w_in_stack, conv_w, row(conv_b), w_gates, row(b_a), row(b_x), row(lam))


def _proj_mlp_kernel(x_ref, y_ref, wo_ref, g_ref, w1_ref, w2_ref, gf_ref, o_ref, x1_ref, h_ref,
                     *, ff_chunk, final_norm):
    x1_ref[...] = x_ref[...] + jnp.dot(y_ref[...].astype(_BF16), wo_ref[...],
                                       preferred_element_type=_F32)
    h_ref[...] = _rms_norm(x1_ref[...], g_ref[...]).astype(_BF16)
    d_ff = w1_ref.shape[1]
    for c in range(0, d_ff, ff_chunk):
        u = jnp.maximum(jnp.dot(h_ref[...], w1_ref[:, c:c + ff_chunk],
                                preferred_element_type=_F32), 0.0)
        x1_ref[...] += jnp.dot((u * u).astype(_BF16), w2_ref[c:c + ff_chunk, :],
                               preferred_element_type=_F32)
    out = x1_ref[...]
    if final_norm:
        out = _rms_norm(out, gf_ref[...])
    o_ref[...] = out


def proj_mlp(x, y, w_o_stack, o_layer, g, w1_stack, w2_stack, layer, g_final, *, batch, seq,
             in_layout, out_layout, final_norm, tm=1024, ff_chunk=1024):
    _, d, d_ff = w1_stack.shape
    dy = w_o_stack.shape[1]
    tm = min(tm, seq)
    ff_chunk = min(ff_chunk, d_ff)
    return pl.pallas_call(
        functools.partial(_proj_mlp_kernel, ff_chunk=ff_chunk, final_norm=final_norm),
        grid=(batch, seq // tm),
        in_specs=[_token_spec(in_layout, tm, d, seq),
                  _token_spec("sb", tm, dy, seq),
                  _resident_layer(w_o_stack.shape, o_layer),
                  _resident((1, d)),
                  _resident_layer(w1_stack.shape, layer),
                  _resident_layer(w2_stack.shape, layer),
                  _resident((1, d))],
        out_specs=_token_spec(out_layout, tm, d, seq),
        out_shape=jax.ShapeDtypeStruct(_token_shape(out_layout, batch, seq, d), _F32),
        scratch_shapes=[pltpu.VMEM((tm, d), _F32),
                        pltpu.VMEM((tm, d), _BF16)],
        compiler_params=pltpu.CompilerParams(dimension_semantics=("parallel", "parallel"),
                                             vmem_limit_bytes=VMEM_LIMIT),
        name="proj_mlp",
    )(x, y, w_o_stack, g.reshape(1, d), w1_stack, w2_stack, g_final.reshape(1, d))


def _block_diag_groups(w_blocks):
    nb, bd, _ = w_blocks.shape
    per = GATE_GROUP // bd
    wg = w_blocks.reshape(nb // per, per, bd, bd)
    eye = jnp.eye(per, dtype=w_blocks.dtype)
    return jnp.einsum("girc,ij->girjc", wg, eye).reshape(nb // per, GATE_GROUP, GATE_GROUP)


def kernel(x, norm_mix, norm_mlp, mlp_w1, mlp_w2, sb_w_qkv, sb_w_o, rg_w_in, rg_conv_w, rg_conv_b,
           rg_w_a, rg_b_a, rg_w_x, rg_b_x, rg_lambda, rg_w_o, norm_final):
    b, s, d = x.shape
    depth = norm_mix.shape[0]
    head_dim = d // SB_HEADS
    q_scale = jnp.concatenate([jnp.full((d,), math.log2(math.e) * head_dim ** -0.5, _F32),
                               jnp.ones((2 * d,), _F32)])

    w_qkv = (sb_w_qkv * q_scale).astype(_BF16)
    sb_w_o, rg_w_in, rg_w_o = sb_w_o.astype(_BF16), rg_w_in.astype(_BF16), rg_w_o.astype(_BF16)
    mlp_w1, mlp_w2 = mlp_w1.astype(_BF16), mlp_w2.astype(_BF16)

    cur, layout = x.reshape(b * s, d), "bs"
    ia = ib = 0
    for layer in range(depth):
        if layer % 2 == 0:
            qkv = norm_proj(cur, norm_mix[layer], w_qkv, ia, batch=b, seq=s, in_layout=layout)
            y = sb_attention(qkv, batch=b, head_dim=head_dim)
            w_o, o_layer = sb_w_o, ia
            ia += 1
        else:
            assert layout == "sb"
            w_gates = (0.5 * jnp.concatenate([_block_diag_groups(rg_w_a[ib]),
                                              _block_diag_groups(rg_w_x[ib])], axis=-1)
                       ).astype(_BF16)
            y = rglru_mixer(cur, norm_mix[layer], rg_w_in, ib, rg_conv_w[ib], rg_conv_b[ib],
                            w_gates, 0.5 * rg_b_a[ib], 0.5 * rg_b_x[ib], rg_lambda[ib], batch=b)
            w_o, o_layer = rg_w_o, ib
            ib += 1
        out_layout = "bs" if layer == depth - 1 else "sb"
        cur = proj_mlp(cur, y, w_o, o_layer, norm_mlp[layer], mlp_w1, mlp_w2, layer, norm_final,
                       batch=b, seq=s, in_layout=layout, out_layout=out_layout,
                       final_norm=(layer == depth - 1))
        layout = out_layout
    return cur.reshape(b, s, d)
```
